```python
import math
import jax, jax.numpy as jnp
from jax import lax
import numpy as np

D_MODEL = 1024
BATCH = 16
SEQ = 2048
DEPTH = 2

M_HEADS = 4
M_QK = 64
M_V = 128
M_WIDTH = M_HEADS * M_V
CONV_W = 4
A_HEADS = 8
A_HEAD_DIM = 64
A_WIDTH = A_HEADS * A_HEAD_DIM
Q_RANK = 256
KV_RANK = 128
IDX_HEADS = 8
IDX_DIM = 64
TOPK_MAX = 256
R_HEADS = 4
R_QK = 128
R_V = 128
R_WIDTH = R_HEADS * R_V
CHUNK = 128
Q_BLOCK = 128
N_BUCKETS = 32
MAX_DISTANCE = 128
ROPE_BASE = 10000.0
EPS = 1e-6
D_FF = -(((-8 * D_MODEL) // 3) // 256) * 256
IN_SPLITS = (M_HEADS * M_QK, M_HEADS * M_QK, M_WIDTH, M_HEADS, M_HEADS, M_WIDTH,
             Q_RANK, IDX_DIM, IDX_HEADS, KV_RANK,
             R_HEADS * R_QK, R_HEADS * R_QK, R_WIDTH, R_WIDTH,
             D_MODEL, D_MODEL, D_MODEL)
D_IN = sum(IN_SPLITS)

kernel_name = 'hybrid_mlstm_dsa_retention_block'


def rms_norm(x, g):
    xf = x.astype(jnp.float32)
    y = xf * lax.rsqrt(jnp.mean(xf * xf, axis=-1, keepdims=True) + EPS)
    return (y * g.astype(jnp.float32)).astype(x.dtype)


def layer_norm(x, g):
    xf = x.astype(jnp.float32)
    mu = jnp.mean(xf, axis=-1, keepdims=True)
    xc = xf - mu
    y = xc * lax.rsqrt(jnp.mean(xc * xc, axis=-1, keepdims=True) + EPS)
    return (y * g.astype(jnp.float32)).astype(x.dtype)


def head_layer_norm(x, g):
    return layer_norm(x, g.reshape(x.shape[-2:]))


def causal_conv(x, w):
    k_w, c = w.shape
    return lax.conv_general_dilated(x, w[:, None, :].astype(x.dtype), window_strides=(1,),
                                    padding=[(k_w - 1, 0)],
                                    dimension_numbers=('NWC', 'WIO', 'NWC'),
                                    feature_group_count=c)


def rotary(x, pos):
    half = x.shape[-1] // 2
    freqs = ROPE_BASE ** (-jnp.linspace(0.0, 1.0, half, dtype=jnp.float32))
    ang = pos[:, None] * freqs[None, :]
    cos = jnp.cos(ang)[:, None, :]
    sin = jnp.sin(ang)[:, None, :]
    xf = x.astype(jnp.float32)
    x1, x2 = xf[..., :half], xf[..., half:]
    return jnp.concatenate([x1 * cos - x2 * sin, x1 * sin + x2 * cos], axis=-1).astype(x.dtype)


def t5_bucket(dist):
    max_exact = N_BUCKETS // 2
    d_f = jnp.maximum(dist, 1).astype(jnp.float32)
    large = max_exact + (jnp.log(d_f / max_exact) / math.log(MAX_DISTANCE / max_exact)
                         * (N_BUCKETS - max_exact)).astype(jnp.int32)
    large = jnp.minimum(large, N_BUCKETS - 1)
    return jnp.where(dist < max_exact, dist, large)


def to_chunks(x):
    b, s, h = x.shape[:3]
    rest = x.shape[3:]
    x = x.reshape((b, s // CHUNK, CHUNK, h) + rest)
    return jnp.transpose(x, (1, 0, 3, 2) + tuple(range(4, x.ndim)))


def from_chunks(x):
    n, b, h, l, d = x.shape
    return jnp.transpose(x, (1, 0, 3, 2, 4)).reshape(b, n * l, h, d)


def mlstm(q, k, v, i_pre, f_pre):
    dtype = v.dtype
    b, s, h, dk = q.shape
    dv = v.shape[-1]
    f32 = jnp.float32
    q = q.astype(f32) * dk ** -0.5
    k = k.astype(f32)
    v = v.astype(f32)
    log_f = jax.nn.log_sigmoid(f_pre.astype(f32))
    log_i = i_pre.astype(f32)
    causal = jnp.tril(jnp.ones((CHUNK, CHUNK), dtype=bool))

    def step(carry, inp):
        c_st, n_st, m_st = carry
        qc, kc, vc, lf, li = inp
        bcum = jnp.cumsum(lf, axis=-1)
        d_log = bcum[..., :, None] - bcum[..., None, :] + li[..., None, :]
        d_log = jnp.where(causal, d_log, -jnp.inf)
        inter = bcum + m_st[..., None]
        m_row = jnp.maximum(inter, jnp.max(d_log, axis=-1))
        w_intra = jnp.exp(d_log - m_row[..., None])
        w_inter = jnp.exp(inter - m_row)
        sc = jnp.einsum('bhld,bhrd->bhlr', qc, kc) * w_intra
        num = (jnp.einsum('bhlr,bhrv->bhlv', sc, vc)
               + w_inter[..., None] * jnp.einsum('bhld,bhdv->bhlv', qc, c_st))
        den = jnp.sum(sc, axis=-1) + w_inter * jnp.einsum('bhld,bhd->bhl', qc, n_st)
        out = num / jnp.maximum(jnp.abs(den), jnp.exp(-m_row))[..., None]
        b_last = bcum[..., -1]
        g_log = b_last[..., None] - bcum + li
        m_new = jnp.maximum(b_last + m_st, jnp.max(g_log, axis=-1))
        wk = jnp.exp(g_log - m_new[..., None])
        decay = jnp.exp(b_last + m_st - m_new)
        c_new = decay[..., None, None] * c_st + jnp.einsum('bhl,bhld,bhlv->bhdv', wk, kc, vc)
        n_new = decay[..., None] * n_st + jnp.einsum('bhl,bhld->bhd', wk, kc)
        return (c_new, n_new, m_new), out

    init = (jnp.zeros((b, h, dk, dv), f32), jnp.zeros((b, h, dk), f32), jnp.zeros((b, h), f32))
    _, out = lax.scan(step, init, (to_chunks(q), to_chunks(k), to_chunks(v),
                                   to_chunks(log_f), to_chunks(log_i)))
    return from_chunks(out).astype(dtype)


def retention(q, k, v):
    dtype = v.dtype
    b, s, h, dk = q.shape
    dv = v.shape[-1]
    f32 = jnp.float32
    q = q.astype(f32)
    k = k.astype(f32) * dk ** -0.5
    v = v.astype(f32)
    log_gamma = jnp.log1p(-jnp.exp2(-5.0 - jnp.arange(h, dtype=f32)))
    pos = jnp.arange(CHUNK, dtype=f32)
    diff = pos[:, None] - pos[None, :]
    intra = jnp.where(diff >= 0, jnp.exp(jnp.maximum(diff, 0.0)[None] * log_gamma[:, None, None]), 0.0)
    q_decay = jnp.exp((pos[None, :] + 1.0) * log_gamma[:, None])
    k_decay = jnp.exp((CHUNK - 1.0 - pos[None, :]) * log_gamma[:, None])
    chunk_decay = jnp.exp(CHUNK * log_gamma)

    def step(state, inp):
        qc, kc, vc = inp
        sc = jnp.einsum('bhld,bhrd->bhlr', qc, kc) * intra
        out = (jnp.einsum('bhlr,bhrv->bhlv', sc, vc)
               + q_decay[..., None] * jnp.einsum('bhld,bhdv->bhlv', qc, state))
        state = chunk_decay[:, None, None] * state + jnp.einsum('bhrd,hr,bhrv->bhdv', kc, k_decay, vc)
        return state, out

    _, out = lax.scan(step, jnp.zeros((b, h, dk, dv), f32), (to_chunks(q), to_chunks(k), to_chunks(v)))
    return from_chunks(out).astype(dtype)


def dsa_attention(q_lat, c_kv, q_idx, k_idx, w_idx, rel_bias):
    b, s = c_kv.shape[:2]
    top_k = min(TOPK_MAX, s // 4)
    key_pos = jnp.arange(s, dtype=jnp.int32)
    f32 = jnp.float32

    def block(start):
        qb = lax.dynamic_slice_in_dim(q_lat, start, Q_BLOCK, axis=1)
        qib = lax.dynamic_slice_in_dim(q_idx, start, Q_BLOCK, axis=1)
        wb = lax.dynamic_slice_in_dim(w_idx, start, Q_BLOCK, axis=1)
        q_pos = start + jnp.arange(Q_BLOCK, dtype=jnp.int32)
        head_sc = jax.nn.relu(jnp.einsum('btgd,bsd->btgs', qib, k_idx).astype(f32))
        idx_sc = jnp.einsum('btg,btgs->bts', wb.astype(f32), head_sc)
        idx_sc = jnp.where(key_pos[None, None, :] <= q_pos[None, :, None], idx_sc, -jnp.inf)
        _, sel = lax.top_k(idx_sc, top_k)
        valid = sel <= q_pos[None, :, None]
        kv_sel = jax.vmap(lambda c, i: c[i])(c_kv, sel)
        logits = jnp.einsum('bthc,btkc->bthk', qb, kv_sel).astype(f32)
        bias = rel_bias.astype(f32)[t5_bucket(q_pos[None, :, None] - sel)]
        logits = logits + jnp.swapaxes(bias, -1, -2)
        logits = jnp.where(valid[:, :, None, :], logits, -jnp.inf)
        probs = jax.nn.softmax(logits, axis=-1).astype(c_kv.dtype)
        return jnp.einsum('bthk,btkc->bthc', probs, kv_sel)

    starts = jnp.arange(s // Q_BLOCK, dtype=jnp.int32) * Q_BLOCK
    out = lax.map(block, starts)
    return jnp.moveaxis(out, 0, 1).reshape(b, s, A_HEADS, KV_RANK)


def hybrid_layer(x, rel_bias, norm1_g, w_in, m_conv, m_ibias, m_fbias, m_norm_g,
                 a_qnorm_g, a_wuq, a_wuq_idx, a_kidx_g, a_kvnorm_g, a_wuk, a_wuv,
                 r_norm_g, p_m, p_a, p_r, w_out, norm2_g, w_gate, w_up, w_down):
    b, s, _ = x.shape
    h = rms_norm(x, norm1_g)
    z = h @ w_in
    split_points = [int(c) for c in np.cumsum(IN_SPLITS)[:-1]]
    (m_q, m_k, m_v, m_i, m_f, m_o, a_cq, a_kidx, a_widx, a_ckv,
     r_q, r_k, r_v, r_g, g_m, g_a, g_r) = jnp.split(z, split_points, axis=-1)

    qk = jax.nn.silu(causal_conv(jnp.concatenate([m_q, m_k], axis=-1), m_conv))
    mq, mk = jnp.split(qk, 2, axis=-1)
    hm = mlstm(mq.reshape(b, s, M_HEADS, M_QK), mk.reshape(b, s, M_HEADS, M_QK),
               m_v.reshape(b, s, M_HEADS, M_V), m_i + m_ibias, m_f + m_fbias)
    hm = head_layer_norm(hm, m_norm_g).reshape(b, s, M_WIDTH) * jax.nn.sigmoid(m_o)

    c_q = rms_norm(a_cq, a_qnorm_g)
    q = (c_q @ a_wuq).reshape(b, s, A_HEADS, A_HEAD_DIM)
    q_idx = (c_q @ a_wuq_idx).reshape(b, s, IDX_HEADS, IDX_DIM)
    k_idx = layer_norm(a_kidx, a_kidx_g)
    w_idx = a_widx * (IDX_HEADS ** -0.5 * IDX_DIM ** -0.5)
    c_kv = rms_norm(a_ckv, a_kvnorm_g)
    q_lat = jnp.einsum('bshd,hdc->bshc', q, a_wuk) * A_HEAD_DIM ** -0.5
    o_lat = dsa_attention(q_lat, c_kv, q_idx, k_idx, w_idx, rel_bias)
    ha = jnp.einsum('bshc,hcd->bshd', o_lat, a_wuv).reshape(b, s, A_WIDTH)

    pos = jnp.arange(s, dtype=jnp.float32)
    rq = rotary(r_q.reshape(b, s, R_HEADS, R_QK), pos)
    rk = rotary(r_k.reshape(b, s, R_HEADS, R_QK), pos)
    hr = retention(rq, rk, r_v.reshape(b, s, R_HEADS, R_V))
    hr = head_layer_norm(hr, r_norm_g).reshape(b, s, R_WIDTH) * jax.nn.silu(r_g)

    y = (jax.nn.sigmoid(g_m) * (hm @ p_m) + jax.nn.sigmoid(g_a) * (ha @ p_a)
         + jax.nn.sigmoid(g_r) * (hr @ p_r))
    x = x + y @ w_out

    h2 = rms_norm(x, norm2_g)
    return x + (jax.nn.silu(h2 @ w_gate) * (h2 @ w_up)) @ w_down


def setup_inputs(seed: int = 0) -> dict:
    key = jax.random.key(seed)
    keys = iter(jax.random.split(key, 32))
    f32 = jnp.float32

    def normal(shape, scale):
        return scale * jax.random.normal(next(keys), shape, f32)

    def gain(shape):
        return 1.0 + normal(shape, 0.02)

    L = DEPTH
    return {
        'x': normal((BATCH, SEQ, D_MODEL), 1.0),
        'rel_bias': normal((N_BUCKETS, A_HEADS), 0.2),
        'final_norm_g': gain((D_MODEL,)),
        'norm1_g': gain((L, D_MODEL)),
        'w_in': normal((L, D_MODEL, D_IN), D_MODEL ** -0.5),
        'm_conv': normal((L, CONV_W, 2 * M_HEADS * M_QK), CONV_W ** -0.5),
        'm_ibias': normal((L, M_HEADS), 0.1),
        'm_fbias': jnp.linspace(3.0, 6.0, M_HEADS, dtype=f32)[None, :] + normal((L, M_HEADS), 0.01),
        'm_norm_g': gain((L, M_WIDTH)),
        'a_qnorm_g': gain((L, Q_RANK)),
        'a_wuq': normal((L, Q_RANK, A_WIDTH), Q_RANK ** -0.5),
        'a_wuq_idx': normal((L, Q_RANK, IDX_HEADS * IDX_DIM), Q_RANK ** -0.5),
        'a_kidx_g': gain((L, IDX_DIM)),
        'a_kvnorm_g': gain((L, KV_RANK)),
        'a_wuk': normal((L, A_HEADS, A_HEAD_DIM, KV_RANK), A_HEAD_DIM ** -0.5),
        'a_wuv': normal((L, A_HEADS, KV_RANK, A_HEAD_DIM), KV_RANK ** -0.5),
        'r_norm_g': gain((L, R_WIDTH)),
        'p_m': normal((L, M_WIDTH, D_MODEL), M_WIDTH ** -0.5),
        'p_a': normal((L, A_WIDTH, D_MODEL), A_WIDTH ** -0.5),
        'p_r': normal((L, R_WIDTH, D_MODEL), R_WIDTH ** -0.5),
        'w_out': normal((L, D_MODEL, D_MODEL), D_MODEL ** -0.5),
        'norm2_g': gain((L, D_MODEL)),
        'w_gate': normal((L, D_MODEL, D_FF), D_MODEL ** -0.5),
        'w_up': normal((L, D_MODEL, D_FF), D_MODEL ** -0.5),
        'w_down': normal((L, D_FF, D_MODEL), D_FF ** -0.5),
    }


def reference(x, rel_bias, final_norm_g, norm1_g, w_in, m_conv, m_ibias, m_fbias, m_norm_g,
              a_qnorm_g, a_wuq, a_wuq_idx, a_kidx_g, a_kvnorm_g, a_wuk, a_wuv, r_norm_g,
              p_m, p_a, p_r, w_out, norm2_g, w_gate, w_up, w_down):
    for l in range(DEPTH):
        x = hybrid_layer(x, rel_bias, norm1_g[l], w_in[l], m_conv[l], m_ibias[l], m_fbias[l],
                         m_norm_g[l], a_qnorm_g[l], a_wuq[l], a_wuq_idx[l], a_kidx_g[l],
                         a_kvnorm_g[l], a_wuk[l], a_wuv[l], r_norm_g[l], p_m[l], p_a[l], p_r[l],
                         w_out[l], norm2_g[l], w_gate[l], w_up[l], w_down[l])
    return rms_norm(x, final_norm_g)
```

```python
import functools
import math

import numpy as np
import jax
import jax.numpy as jnp
from jax import lax
from jax.experimental import pallas as pl
from jax.experimental.pallas import tpu as pltpu

F32 = jnp.float32
BF16 = jnp.bfloat16
I32 = jnp.int32

D_MODEL = 1024
M_HEADS, M_QK, M_V = 4, 64, 128
M_WIDTH = M_HEADS * M_V
CONV_W = 4
A_HEADS, A_HEAD_DIM = 8, 64
A_WIDTH = A_HEADS * A_HEAD_DIM
Q_RANK, KV_RANK = 256, 128
IDX_HEADS, IDX_DIM = 8, 64
TOPK_MAX = 256
R_HEADS, R_QK, R_V = 4, 128, 128
R_WIDTH = R_HEADS * R_V
CHUNK = 128
Q_BLOCK = 128
N_BUCKETS = 32
MAX_DISTANCE = 128
ROPE_BASE = 10000.0
EPS = 1e-6
D_FF = -(((-8 * D_MODEL) // 3) // 256) * 256
IN_SPLITS = (M_HEADS * M_QK, M_HEADS * M_QK, M_WIDTH, M_HEADS, M_HEADS, M_WIDTH,
             Q_RANK, IDX_DIM, IDX_HEADS, KV_RANK,
             R_HEADS * R_QK, R_HEADS * R_QK, R_WIDTH, R_WIDTH,
             D_MODEL, D_MODEL, D_MODEL)

LANES = 128
SUBLANES = 8
VMEM_LIMIT = 56 * 1024 * 1024

Z_MQK = 0
Z_MV = 512
Z_MO = 1024
Z_ACQ = 1536
Z_ACKV = 1792
Z_SMALL = 1920
Z_RQ, Z_RK, Z_RV, Z_RG = 2048, 2560, 3072, 3584
Z_GM, Z_GA, Z_GR = 4096, 5120, 6144
Z_WIDTH = 7168
SM_KIDX, SM_I, SM_F, SM_W = 0, 64, 68, 72

KEY_TILE = 256
INT_MIN = -2 ** 31


def _cparams(sem):
    return pltpu.CompilerParams(dimension_semantics=sem, vmem_limit_bytes=VMEM_LIMIT)


def _resident(shape):
    nd = len(shape)
    return pl.BlockSpec(shape, lambda *_: (0,) * nd, pipeline_mode=pl.Buffered(1))


def _rms(x, g):
    return x * lax.rsqrt(jnp.mean(x * x, axis=-1, keepdims=True) + EPS) * g


def _group_norm(x, g):
    mu = jnp.mean(x, axis=-1, keepdims=True)
    xc = x - mu
    return xc * lax.rsqrt(jnp.mean(xc * xc, axis=-1, keepdims=True) + EPS) * g


def _dot(a, b):
    return jnp.dot(a, b, preferred_element_type=F32)


def _dot_nt(a, b):
    return lax.dot_general(a, b, (((1,), (1,)), ((), ())), preferred_element_type=F32)


def _inproj_kernel(x_ref, g_ref, w_ref, z_ref, h_scr):
    @pl.when(pl.program_id(1) == 0)
    def _():
        h_scr[...] = _rms(x_ref[...], g_ref[...]).astype(BF16)

    z_ref[...] = _dot(h_scr[...], w_ref[...])


def _inproj(x2, g, w, tm, tn):
    t = x2.shape[0]
    return pl.pallas_call(
        _inproj_kernel,
        grid=(t // tm, Z_WIDTH // tn),
        in_specs=[pl.BlockSpec((tm, D_MODEL), lambda i, j: (i, 0)),
                  pl.BlockSpec((1, D_MODEL), lambda i, j: (0, 0)),
                  pl.BlockSpec((D_MODEL, tn), lambda i, j: (0, j))],
        out_specs=pl.BlockSpec((tm, tn), lambda i, j: (i, j)),
        out_shape=jax.ShapeDtypeStruct((t, Z_WIDTH), F32),
        scratch_shapes=[pltpu.VMEM((tm, D_MODEL), BF16)],
        compiler_params=_cparams(("parallel", "arbitrary")),
        name="inproj",
    )(x2, g, w)


def _mlstm_kernel(qk_ref, v_ref, o_ref, sm_ref, conv_ref, gb_ref, ng_ref, out_ref,
                  xbuf, cst, mst):
    L = CHUNK
    c = pl.program_id(1)

    @pl.when(c == 0)
    def _():
        xbuf[0:SUBLANES, :] = jnp.zeros((SUBLANES, 2 * M_HEADS * M_QK), F32)
        cst[...] = jnp.zeros_like(cst)
        mst[...] = jnp.zeros_like(mst)

    x = qk_ref[...]
    xbuf[SUBLANES:SUBLANES + L, :] = x
    w = conv_ref[...]
    acc = w[CONV_W - 1:CONV_W, :] * x
    for j in range(CONV_W - 1):
        off = SUBLANES - (CONV_W - 1) + j
        acc = acc + w[j:j + 1, :] * xbuf[off:off + L, :]
    xbuf[0:SUBLANES, :] = x[L - SUBLANES:L, :]
    qk = acc * jax.nn.sigmoid(acc)
    kt_all = qk[:, M_HEADS * M_QK:].T

    g = sm_ref[...] + gb_ref[0:1, :]
    lane = lax.broadcasted_iota(I32, (L, LANES), 1)
    xg = jnp.where((lane >= SM_F) & (lane < SM_F + M_HEADS), jax.nn.log_sigmoid(g), g)
    r_i = lax.broadcasted_iota(I32, (L, L), 0)
    c_i = lax.broadcasted_iota(I32, (L, L), 1)
    causal = r_i >= c_i
    bc = jnp.dot(causal.astype(F32), xg, preferred_element_type=F32,
                 precision=lax.Precision.HIGHEST)
    xg_t = xg.T
    bc_t = bc.T
    ones = jnp.ones((L, M_V), BF16)

    for h in range(M_HEADS):
        bc_c = bc[:, SM_F + h:SM_F + h + 1]
        li_r = xg_t[SM_I + h:SM_I + h + 1, :]
        bc_r = bc_t[SM_F + h:SM_F + h + 1, :]
        m_prev = mst[h:h + 1, 0:1]
        b_last = bc_c[L - 1:L, :]
        b_r = li_r - bc_r
        d_log = jnp.where(causal, bc_c + b_r, -jnp.inf)
        inter = bc_c + m_prev
        m_row = jnp.maximum(inter, jnp.max(d_log, axis=-1, keepdims=True))
        w_intra = jnp.exp(d_log - m_row)
        w_inter = jnp.exp(inter - m_row)

        qb = (qk[:, h * M_QK:(h + 1) * M_QK] * (M_QK ** -0.5)).astype(BF16)
        kt = kt_all[h * M_QK:(h + 1) * M_QK, :]
        vb = v_ref[:, h * M_V:(h + 1) * M_V].astype(BF16)
        vaug = jnp.concatenate([vb, ones], axis=1)
        caug = cst[h]
        sc = _dot(qb, kt.astype(BF16)) * w_intra
        num = _dot(sc.astype(BF16), vaug) + w_inter * _dot(qb, caug.astype(BF16))
        den = num[:, M_V:M_V + 1]
        out = num[:, :M_V] / jnp.maximum(jnp.abs(den), jnp.exp(-m_row))

        g_r = b_last + b_r
        m_new = jnp.maximum(b_last + m_prev, jnp.max(g_r, axis=-1, keepdims=True))
        wk_r = jnp.exp(g_r - m_new)
        decay = jnp.exp(b_last + m_prev - m_new)
        cst[h] = decay * caug + _dot((kt * wk_r).astype(BF16), vaug)
        mst[h:h + 1, :] = jnp.broadcast_to(m_new, (1, LANES))

        y = _group_norm(out, ng_ref[0:1, h * M_V:(h + 1) * M_V])
        gate = jax.nn.sigmoid(o_ref[:, h * M_V:(h + 1) * M_V])
        out_ref[:, h * M_V:(h + 1) * M_V] = (y * gate).astype(BF16)


def _mlstm(z, conv_w, gbias, ng, b, s):
    nc = s // CHUNK
    row = lambda bi, ci: bi * nc + ci
    w512 = 2 * M_HEADS * M_QK
    return pl.pallas_call(
        _mlstm_kernel,
        grid=(b, nc),
        in_specs=[pl.BlockSpec((CHUNK, w512), lambda bi, ci: (row(bi, ci), Z_MQK // w512)),
                  pl.BlockSpec((CHUNK, M_WIDTH), lambda bi, ci: (row(bi, ci), Z_MV // M_WIDTH)),
                  pl.BlockSpec((CHUNK, M_WIDTH), lambda bi, ci: (row(bi, ci), Z_MO // M_WIDTH)),
                  pl.BlockSpec((CHUNK, LANES), lambda bi, ci: (row(bi, ci), Z_SMALL // LANES)),
                  _resident((SUBLANES, w512)),
                  _resident((SUBLANES, LANES)),
                  _resident((SUBLANES, M_WIDTH))],
        out_specs=pl.BlockSpec((CHUNK, M_WIDTH), lambda bi, ci: (row(bi, ci), 0)),
        out_shape=jax.ShapeDtypeStruct((b * s, M_WIDTH), BF16),
        scratch_shapes=[pltpu.VMEM((SUBLANES + CHUNK, w512), F32),
                        pltpu.VMEM((M_HEADS, M_QK, 2 * M_V), F32),
                        pltpu.VMEM((SUBLANES, LANES), F32)],
        compiler_params=_cparams(("parallel", "arbitrary")),
        name="mlstm",
    )(z, z, z, z, conv_w, gbias, ng)


def _ret_kernel(q_ref, k_ref, v_ref, g_ref, cos_ref, sin_ref, intra_ref, qdec_ref, kdec_ref,
                cdec_ref, ng_ref, out_ref, st):
    c = pl.program_id(1)

    @pl.when(c == 0)
    def _():
        st[...] = jnp.zeros_like(st)

    cosf = cos_ref[...]
    sinf = sin_ref[...]
    half = R_QK // 2
    for h in range(R_HEADS):
        sl = slice(h * R_QK, (h + 1) * R_QK)
        q = q_ref[:, sl]
        k = k_ref[:, sl]
        qr = (q * cosf + pltpu.roll(q, half, 1) * sinf).astype(BF16)
        kr = (k * cosf + pltpu.roll(k, half, 1) * sinf) * (R_QK ** -0.5)
        kt = kr.T
        vb = v_ref[:, sl].astype(BF16)
        state = st[h]
        sc = _dot(qr, kt.astype(BF16)) * intra_ref[h]
        out = _dot(sc.astype(BF16), vb) + qdec_ref[:, h:h + 1] * _dot(qr, state.astype(BF16))
        st[h] = cdec_ref[h:h + 1, :] * state + _dot((kt * kdec_ref[h:h + 1, :]).astype(BF16), vb)
        y = _group_norm(out, ng_ref[0:1, sl])
        gt = g_ref[:, sl]
        out_ref[:, sl] = (y * (gt * jax.nn.sigmoid(gt))).astype(BF16)


def _retention(z, cosf, sinf, intra, qdec, kdec, cdec, ng, b, s):
    nc = s // CHUNK
    row = lambda bi, ci: bi * nc + ci
    zspec = lambda off: pl.BlockSpec((CHUNK, R_WIDTH), lambda bi, ci: (row(bi, ci), off // R_WIDTH))
    return pl.pallas_call(
        _ret_kernel,
        grid=(b, nc),
        in_specs=[zspec(Z_RQ), zspec(Z_RK), zspec(Z_RV), zspec(Z_RG),
                  pl.BlockSpec((CHUNK, R_QK), lambda bi, ci: (ci, 0)),
                  pl.BlockSpec((CHUNK, R_QK), lambda bi, ci: (ci, 0)),
                  _resident((R_HEADS, CHUNK, CHUNK)),
                  _resident((CHUNK, LANES)),
                  _resident((SUBLANES, CHUNK)),
                  _resident((SUBLANES, R_V)),
                  _resident((SUBLANES, R_WIDTH))],
        out_specs=pl.BlockSpec((CHUNK, R_WIDTH), lambda bi, ci: (row(bi, ci), 0)),
        out_shape=jax.ShapeDtypeStruct((b * s, R_WIDTH), BF16),
        scratch_shapes=[pltpu.VMEM((R_HEADS, R_QK, R_V), F32)],
        compiler_params=_cparams(("parallel", "arbitrary")),
        name="retention",
    )(z, z, z, z, cosf, sinf, intra, qdec, kdec, cdec, ng)


def _dsa_prep_kernel(cq_ref, ckv_ref, sm_ref, qn_ref, kvn_ref, kig_ref, wuq_ref, wuqi_ref,
                     wuk_ref, qlat_ref, qidx_ref, kidx_ref, ckvn_ref):
    c_q = _rms(cq_ref[...], qn_ref[0:1, :]).astype(BF16)
    q = _dot(c_q, wuq_ref[...])
    for h in range(A_HEADS):
        qh = q[:, h * A_HEAD_DIM:(h + 1) * A_HEAD_DIM].astype(BF16)
        qlat_ref[h] = (_dot(qh, wuk_ref[h]) * (A_HEAD_DIM ** -0.5)).astype(BF16)
    for h in range(IDX_HEADS):
        qidx_ref[h] = _dot(c_q, wuqi_ref[h]).astype(BF16)
    sm = sm_ref[...]
    lane = lax.broadcasted_iota(I32, sm.shape, 1)
    inside = lane < IDX_DIM
    mu = jnp.sum(jnp.where(inside, sm, 0.0), axis=-1, keepdims=True) * (1.0 / IDX_DIM)
    xc = jnp.where(inside, sm - mu, 0.0)
    var = jnp.sum(xc * xc, axis=-1, keepdims=True) * (1.0 / IDX_DIM)
    kidx_ref[...] = (xc * lax.rsqrt(var + EPS) * kig_ref[0:1, :]).astype(BF16)
    ckvn_ref[...] = _rms(ckv_ref[...], kvn_ref[0:1, :]).astype(BF16)


def _dsa_prep(z, qn, kvn, kig, wuq, wuqi, wuk, tm):
    t = z.shape[0]
    hspec = pl.BlockSpec((A_HEADS, tm, LANES), lambda i: (0, i, 0))
    return pl.pallas_call(
        _dsa_prep_kernel,
        grid=(t // tm,),
        in_specs=[pl.BlockSpec((tm, Q_RANK), lambda i: (i, Z_ACQ // Q_RANK)),
                  pl.BlockSpec((tm, KV_RANK), lambda i: (i, Z_ACKV // KV_RANK)),
                  pl.BlockSpec((tm, LANES), lambda i: (i, Z_SMALL // LANES)),
                  _resident((SUBLANES, Q_RANK)),
                  _resident((SUBLANES, KV_RANK)),
                  _resident((SUBLANES, LANES)),
                  _resident((Q_RANK, A_WIDTH)),
                  _resident((IDX_HEADS, Q_RANK, LANES)),
                  _resident((A_HEADS, A_HEAD_DIM, KV_RANK))],
        out_specs=[hspec, hspec,
                   pl.BlockSpec((tm, LANES), lambda i: (i, 0)),
                   pl.BlockSpec((tm, KV_RANK), lambda i: (i, 0))],
        out_shape=[jax.ShapeDtypeStruct((A_HEADS, t, KV_RANK), BF16),
                   jax.ShapeDtypeStruct((IDX_HEADS, t, LANES), BF16),
                   jax.ShapeDtypeStruct((t, LANES), BF16),
                   jax.ShapeDtypeStruct((t, KV_RANK), BF16)],
        compiler_params=_cparams(("parallel",)),
        name="dsa_prep",
    )(z, z, z, qn, kvn, kig, wuq, wuqi, wuk)


def _dsa_kernel(qidx_ref, qlat_ref, sm_ref, kidx_ref, ckv_ref, bias_ref, wuv_ref, out_ref,
                keys_scr, lg_scr, mx_scr, l_scr, acc_scr, p_scr, *, top_k, n_keys):
    QB = Q_BLOCK
    KT = KEY_TILE
    HQ = A_HEADS * QB
    qblk = pl.program_id(1)
    nk = (qblk * QB + QB + KT - 1) // KT
    kf = float(top_k)

    row_pos = qblk * QB + lax.broadcasted_iota(I32, (QB, KT), 0)
    col_off = lax.broadcasted_iota(I32, (QB, KT), 1)

    qi = qidx_ref[...].reshape(HQ, LANES)
    sm = sm_ref[...]
    w_scale = IDX_HEADS ** -0.5 * IDX_DIM ** -0.5
    w_cols = [sm[:, SM_W + g:SM_W + g + 1] * w_scale for g in range(IDX_HEADS)]

    def score_body(kt, carry):
        s = _dot_nt(qi, kidx_ref[0, kt])
        tot = jnp.zeros((QB, KT), F32)
        for g in range(IDX_HEADS):
            tot = tot + w_cols[g] * jnp.maximum(s[g * QB:(g + 1) * QB, :], 0.0)
        key_pos = kt * KT + col_off
        tot = jnp.where(key_pos <= row_pos, tot + 0.0, -jnp.inf)
        bits = pltpu.bitcast(tot, I32)
        keys_scr[kt] = bits ^ ((bits >> 31) & 0x7FFFFFFF)
        return carry

    lax.fori_loop(0, nk, score_body, 0)

    def count(pred):
        def body(kt, acc):
            return acc + jnp.where(pred(keys_scr[kt], kt), 1.0, 0.0)
        acc = lax.fori_loop(0, nk, body, jnp.zeros((QB, KT), F32))
        return jnp.sum(acc, axis=-1, keepdims=True)

    def count_ge(cand):
        cb = jnp.broadcast_to(cand, (QB, KT))
        return count(lambda keys, kt: keys >= cb)

    thr = jnp.where(count_ge(jnp.zeros((QB, 1), I32)) >= kf, 0, INT_MIN).astype(I32)

    def bit_body(i, thr):
        cand = thr | (jnp.int32(1) << (30 - i))
        return jnp.where(count_ge(cand) >= kf, cand, thr)

    thr = lax.fori_loop(0, 31, bit_body, thr)
    thr_b = jnp.broadcast_to(thr, (QB, KT))

    p_scr[...] = jnp.full((QB, LANES), n_keys, I32)
    over = jnp.max(jnp.where(count_ge(thr) > kf, 1.0, 0.0))

    @pl.when(over > 0.0)
    def _():
        need = kf - count(lambda keys, kt: keys > thr_b)

        def idx_body(i, p):
            cand = p | (jnp.int32(1) << (n_keys.bit_length() - 1 - i))
            cb = jnp.broadcast_to(cand, (QB, KT))
            cnt = count(lambda keys, kt: (keys == thr_b) & (kt * KT + col_off < cb))
            return jnp.where(cnt < need, cand, p)

        p = lax.fori_loop(0, n_keys.bit_length(), idx_body, jnp.zeros((QB, 1), I32))
        p_scr[...] = jnp.broadcast_to(p, (QB, LANES))

    tie_hi = jnp.concatenate([p_scr[...]] * (KT // LANES), axis=1)

    ql = qlat_ref[...].reshape(HQ, KV_RANK)
    mx_scr[...] = jnp.full((HQ, LANES), -jnp.inf, F32)

    def pass_a(kt, carry):
        logits = _dot_nt(ql, ckv_ref[0, kt])
        keys = keys_scr[kt]
        key_pos = kt * KT + col_off
        sel = (keys > thr_b) | ((keys == thr_b) & (key_pos <= tie_hi))
        sel = sel & (key_pos <= row_pos)
        didx = jnp.minimum(qblk - (KT // QB) * kt, 3)
        for h in range(A_HEADS):
            rows = slice(h * QB, (h + 1) * QB)
            lh = jnp.where(sel, logits[rows, :] + bias_ref[didx, h], -jnp.inf)
            lg_scr[kt, rows, :] = lh
            m = lh[:, :LANES]
            for j in range(1, KT // LANES):
                m = jnp.maximum(m, lh[:, j * LANES:(j + 1) * LANES])
            mx_scr[rows, :] = jnp.maximum(mx_scr[rows, :], m)
        return carry

    lax.fori_loop(0, nk, pass_a, 0)
    mx_scr[...] = jnp.broadcast_to(jnp.max(mx_scr[...], axis=-1, keepdims=True), (HQ, LANES))

    l_scr[...] = jnp.zeros((HQ, LANES), F32)
    acc_scr[...] = jnp.zeros((HQ, KV_RANK), F32)

    def pass_b(kt, carry):
        m = mx_scr[...]
        lg = lg_scr[kt]
        ps = [jnp.exp(lg[:, j * LANES:(j + 1) * LANES] - m) for j in range(KT // LANES)]
        l_scr[...] = l_scr[...] + functools.reduce(lambda a, b: a + b, ps)
        p = jnp.concatenate(ps, axis=1).astype(BF16)
        acc_scr[...] = acc_scr[...] + _dot(p, ckv_ref[0, kt])
        return carry

    lax.fori_loop(0, nk, pass_b, 0)

    o = acc_scr[...] / jnp.sum(l_scr[...], axis=-1, keepdims=True)
    ha = jnp.zeros((QB, A_WIDTH), F32)
    for h in range(A_HEADS):
        ha = ha + _dot(o[h * QB:(h + 1) * QB, :].astype(BF16), wuv_ref[h])
    out_ref[...] = ha.astype(BF16)


def _dsa(qidx, qlat, z, kidx, ckv, bias_tab, wuv_pad, b, s):
    nq = s // Q_BLOCK
    nt = s // KEY_TILE
    top_k = min(TOPK_MAX, s // 4)
    row = lambda bi, qi: bi * nq + qi
    hspec = pl.BlockSpec((A_HEADS, Q_BLOCK, LANES), lambda bi, qi: (0, row(bi, qi), 0))
    kspec = pl.BlockSpec((1, nt, KEY_TILE, LANES), lambda bi, qi: (bi, 0, 0, 0))
    hq = A_HEADS * Q_BLOCK
    return pl.pallas_call(
        functools.partial(_dsa_kernel, top_k=top_k, n_keys=s),
        grid=(b, nq),
        in_specs=[hspec, hspec,
                  pl.BlockSpec((Q_BLOCK, LANES), lambda bi, qi: (row(bi, qi), Z_SMALL // LANES)),
                  kspec, kspec,
                  _resident((4, A_HEADS, Q_BLOCK, KEY_TILE)),
                  _resident((A_HEADS, KV_RANK, A_WIDTH))],
        out_specs=pl.BlockSpec((Q_BLOCK, A_WIDTH), lambda bi, qi: (row(bi, qi), 0)),
        out_shape=jax.ShapeDtypeStruct((b * s, A_WIDTH), BF16),
        scratch_shapes=[pltpu.VMEM((nt, Q_BLOCK, KEY_TILE), I32),
                        pltpu.VMEM((nt, hq, KEY_TILE), F32),
                        pltpu.VMEM((hq, LANES), F32),
                        pltpu.VMEM((hq, LANES), F32),
                        pltpu.VMEM((hq, KV_RANK), F32),
                        pltpu.VMEM((Q_BLOCK, LANES), I32)],
        compiler_params=_cparams(("parallel", "arbitrary")),
        name="dsa",
    )(qidx, qlat, z, kidx.reshape(b, nt, KEY_TILE, LANES), ckv.reshape(b, nt, KEY_TILE, LANES),
      bias_tab, wuv_pad)


FF_CHUNK = 256


def _merge_ffn_kernel(x_ref, hm_ref, ha_ref, hr_ref, gm_ref, ga_ref, gr_ref, pm_ref, pa_ref,
                      pr_ref, wo_ref, n2_ref, wg_ref, wu_ref, wd_ref, fg_ref, out_ref, *, final):
    y = (jax.nn.sigmoid(gm_ref[...]) * _dot(hm_ref[...], pm_ref[...])
         + jax.nn.sigmoid(ga_ref[...]) * _dot(ha_ref[...], pa_ref[...])
         + jax.nn.sigmoid(gr_ref[...]) * _dot(hr_ref[...], pr_ref[...]))
    x1 = x_ref[...] + _dot(y.astype(BF16), wo_ref[...])
    h2 = _rms(x1, n2_ref[0:1, :]).astype(BF16)
    acc = x1
    for c in range(D_FF // FF_CHUNK):
        cs = slice(c * FF_CHUNK, (c + 1) * FF_CHUNK)
        gate = _dot(h2, wg_ref[:, cs])
        up = _dot(h2, wu_ref[:, cs])
        act = (gate * jax.nn.sigmoid(gate) * up).astype(BF16)
        acc = acc + _dot(act, wd_ref[cs, :])
    if final:
        acc = _rms(acc, fg_ref[0:1, :])
    out_ref[...] = acc


def _merge_ffn(x2, hm, ha, hr, z, pm, pa, pr, wo, n2, wg, wu, wd, fg, tm, final):
    t = x2.shape[0]
    rowspec = lambda w, cb=0: pl.BlockSpec((tm, w), lambda i: (i, cb))
    return pl.pallas_call(
        functools.partial(_merge_ffn_kernel, final=final),
        grid=(t // tm,),
        in_specs=[rowspec(D_MODEL), rowspec(M_WIDTH), rowspec(A_WIDTH), rowspec(R_WIDTH),
                  rowspec(D_MODEL, Z_GM // D_MODEL), rowspec(D_MODEL, Z_GA // D_MODEL),
                  rowspec(D_MODEL, Z_GR // D_MODEL),
                  _resident((M_WIDTH, D_MODEL)), _resident((A_WIDTH, D_MODEL)),
                  _resident((R_WIDTH, D_MODEL)), _resident((D_MODEL, D_MODEL)),
                  _resident((SUBLANES, D_MODEL)),
                  _resident((D_MODEL, D_FF)), _resident((D_MODEL, D_FF)),
                  _resident((D_FF, D_MODEL)), _resident((SUBLANES, D_MODEL))],
        out_specs=rowspec(D_MODEL),
        out_shape=jax.ShapeDtypeStruct((t, D_MODEL), F32),
        compiler_params=_cparams(("parallel",)),
        name="merge_ffn",
    )(x2, hm, ha, hr, z, z, z, pm, pa, pr, wo, n2, wg, wu, wd, fg)


def _pad_rows(v, rows=SUBLANES):
    v = jnp.atleast_2d(v).astype(F32)
    return jnp.pad(v, ((0, rows - v.shape[0]), (0, 0)))


def _t5_bucket(dist):
    max_exact = N_BUCKETS // 2
    d_f = jnp.maximum(dist, 1).astype(F32)
    large = max_exact + (jnp.log(d_f / max_exact) / math.log(MAX_DISTANCE / max_exact)
                         * (N_BUCKETS - max_exact)).astype(I32)
    large = jnp.minimum(large, N_BUCKETS - 1)
    return jnp.where(dist < max_exact, dist, large)


def _bias_table(rel_bias):
    i = jnp.arange(Q_BLOCK, dtype=I32)[:, None]
    j = jnp.arange(KEY_TILE, dtype=I32)[None, :]
    tabs = []
    for d in range(4):
        dist = jnp.maximum(i + d * Q_BLOCK - j, 0)
        tabs.append(rel_bias.astype(F32)[_t5_bucket(dist)])
    return jnp.transpose(jnp.stack(tabs), (0, 3, 1, 2))


def _rotary_tables(s):
    half = R_QK // 2
    freqs = ROPE_BASE ** (-jnp.linspace(0.0, 1.0, half, dtype=F32))
    ang = jnp.arange(s, dtype=F32)[:, None] * freqs[None, :]
    cos, sin = jnp.cos(ang), jnp.sin(ang)
    return jnp.concatenate([cos, cos], axis=1), jnp.concatenate([-sin, sin], axis=1)


def _retention_tables():
    h = R_HEADS
    log_gamma = jnp.log1p(-jnp.exp2(-5.0 - jnp.arange(h, dtype=F32)))
    pos = jnp.arange(CHUNK, dtype=F32)
    diff = pos[:, None] - pos[None, :]
    intra = jnp.where(diff >= 0, jnp.exp(jnp.maximum(diff, 0.0)[None] * log_gamma[:, None, None]), 0.0)
    q_decay = jnp.exp((pos[None, :] + 1.0) * log_gamma[:, None])
    k_decay = jnp.exp((CHUNK - 1.0 - pos[None, :]) * log_gamma[:, None])
    chunk_decay = jnp.exp(CHUNK * log_gamma)
    qdec = jnp.pad(q_decay.T, ((0, 0), (0, LANES - h)))
    kdec = _pad_rows(k_decay)
    cdec = _pad_rows(jnp.broadcast_to(chunk_decay[:, None], (h, R_V)))
    return intra, qdec, kdec, cdec


def _layout_w_in(w_in):
    parts = jnp.split(w_in, [int(c) for c in np.cumsum(IN_SPLITS)[:-1]], axis=-1)
    (m_q, m_k, m_v, m_i, m_f, m_o, a_cq, a_kidx, a_widx, a_ckv,
     r_q, r_k, r_v, r_g, g_m, g_a, g_r) = parts
    small = jnp.concatenate([a_kidx, m_i, m_f, a_widx], axis=-1)
    small = jnp.pad(small, ((0, 0), (0, LANES - small.shape[-1])))
    w = jnp.concatenate([m_q, m_k, m_v, m_o, a_cq, a_ckv, small, r_q, r_k, r_v, r_g,
                         g_m, g_a, g_r], axis=-1)
    assert w.shape[-1] == Z_WIDTH
    return w.astype(BF16)


def _row_tile(t, want):
    while t % want:
        want //= 2
    return want


def kernel(x, rel_bias, final_norm_g, norm1_g, w_in, m_conv, m_ibias, m_fbias, m_norm_g, a_qnorm_g, a_wuq, a_wuq_idx, a_kidx_g, a_kvnorm_g, a_wuk, a_wuv, r_norm_g, p_m, p_a, p_r, w_out, norm2_g, w_gate, w_up, w_down):
    b, s, d = x.shape
    depth = w_in.shape[0]
    assert d == D_MODEL and s % KEY_TILE == 0
    t = b * s
    tm_in = _row_tile(t, 1024)
    tm = _row_tile(t, 512)

    bias_tab = _bias_table(rel_bias)
    cosf, sinf = _rotary_tables(s)
    intra, qdec, kdec, cdec = _retention_tables()
    fg = _pad_rows(final_norm_g)

    x2 = x.reshape(t, d)
    for l in range(depth):
        z = _inproj(x2, norm1_g[l][None, :], _layout_w_in(w_in[l]), tm_in, 1024)

        gbias = jnp.zeros((LANES,), F32)
        gbias = gbias.at[SM_I:SM_I + M_HEADS].set(m_ibias[l]).at[SM_F:SM_F + M_HEADS].set(m_fbias[l])
        hm = _mlstm(z, _pad_rows(m_conv[l]), _pad_rows(gbias), _pad_rows(m_norm_g[l]), b, s)

        hr = _retention(z, cosf, sinf, intra, qdec, kdec, cdec, _pad_rows(r_norm_g[l]), b, s)

        wuqi = a_wuq_idx[l].reshape(Q_RANK, IDX_HEADS, IDX_DIM).transpose(1, 0, 2)
        wuqi = jnp.pad(wuqi, ((0, 0), (0, 0), (0, LANES - IDX_DIM))).astype(BF16)
        kig = jnp.pad(a_kidx_g[l], (0, LANES - IDX_DIM))
        qlat, qidx, kidx, ckv = _dsa_prep(z, _pad_rows(a_qnorm_g[l]), _pad_rows(a_kvnorm_g[l]),
                                          _pad_rows(kig), a_wuq[l].astype(BF16), wuqi,
                                          a_wuk[l].astype(BF16), tm)
        eye = jnp.eye(A_HEADS, dtype=F32)
        wuv_pad = (a_wuv[l][:, :, None, :] * eye[:, None, :, None]).reshape(A_HEADS, KV_RANK, A_WIDTH)
        ha = _dsa(qidx, qlat, z, kidx, ckv, bias_tab, wuv_pad.astype(BF16), b, s)

        x2 = _merge_ffn(x2, hm, ha, hr, z, p_m[l].astype(BF16), p_a[l].astype(BF16),
                        p_r[l].astype(BF16), w_out[l].astype(BF16), _pad_rows(norm2_g[l]),
                        w_gate[l].astype(BF16), w_up[l].astype(BF16), w_down[l].astype(BF16),
                        fg, tm, final=(l == depth - 1))
    return x2.reshape(b, s, d)
```

```python
import functools
import math

import numpy as np
import jax
import jax.numpy as jnp
from jax import lax
from jax.experimental import pallas as pl
from jax.experimental.pallas import tpu as pltpu

F32 = jnp.float32
BF16 = jnp.bfloat16
I32 = jnp.int32

D_MODEL = 1024
M_HEADS, M_QK, M_V = 4, 64, 128
M_WIDTH = M_HEADS * M_V
CONV_W = 4
A_HEADS, A_HEAD_DIM = 8, 64
A_WIDTH = A_HEADS * A_HEAD_DIM
Q_RANK, KV_RANK = 256, 128
IDX_HEADS, IDX_DIM = 8, 64
TOPK_MAX = 256
R_HEADS, R_QK, R_V = 4, 128, 128
R_WIDTH = R_HEADS * R_V
CHUNK = 128
Q_BLOCK = 128
N_BUCKETS = 32
MAX_DISTANCE = 128
ROPE_BASE = 10000.0
EPS = 1e-6
D_FF = -(((-8 * D_MODEL) // 3) // 256) * 256
IN_SPLITS = (M_HEADS * M_QK, M_HEADS * M_QK, M_WIDTH, M_HEADS, M_HEADS, M_WIDTH,
             Q_RANK, IDX_DIM, IDX_HEADS, KV_RANK,
             R_HEADS * R_QK, R_HEADS * R_QK, R_WIDTH, R_WIDTH,
             D_MODEL, D_MODEL, D_MODEL)

LANES = 128
SUBLANES = 8
VMEM_LIMIT = 56 * 1024 * 1024

Z_MQK = 0
Z_MV = 512
Z_MO = 1024
Z_ACQ = 1536
Z_ACKV = 1792
Z_SMALL = 1920
Z_RQ, Z_RK, Z_RV, Z_RG = 2048, 2560, 3072, 3584
Z_GM, Z_GA, Z_GR = 4096, 5120, 6144
Z_WIDTH = 7168
SM_KIDX, SM_I, SM_F, SM_W = 0, 64, 68, 72

KEY_TILE = 256
INT_MIN = -2 ** 31


def _cparams(sem):
    return pltpu.CompilerParams(dimension_semantics=sem, vmem_limit_bytes=VMEM_LIMIT)


def _resident(shape):
    nd = len(shape)
    return pl.BlockSpec(shape, lambda *_: (0,) * nd, pipeline_mode=pl.Buffered(1))


def _rms(x, g):
    return x * lax.rsqrt(jnp.mean(x * x, axis=-1, keepdims=True) + EPS) * g


def _group_norm(x, g):
    mu = jnp.mean(x, axis=-1, keepdims=True)
    xc = x - mu
    return xc * lax.rsqrt(jnp.mean(xc * xc, axis=-1, keepdims=True) + EPS) * g


def _dot(a, b):
    return jnp.dot(a, b, preferred_element_type=F32)


def _inproj_kernel(x_ref, g_ref, w_ref, z_ref, h_scr):
    @pl.when(pl.program_id(1) == 0)
    def _():
        h_scr[...] = _rms(x_ref[...], g_ref[...]).astype(BF16)

    z_ref[...] = _dot(h_scr[...], w_ref[...])


def _inproj(x2, g, w, tm, tn):
    t = x2.shape[0]
    return pl.pallas_call(
        _inproj_kernel,
        grid=(t // tm, Z_WIDTH // tn),
        in_specs=[pl.BlockSpec((tm, D_MODEL), lambda i, j: (i, 0)),
                  pl.BlockSpec((1, D_MODEL), lambda i, j: (0, 0)),
                  pl.BlockSpec((D_MODEL, tn), lambda i, j: (0, j))],
        out_specs=pl.BlockSpec((tm, tn), lambda i, j: (i, j)),
        out_shape=jax.ShapeDtypeStruct((t, Z_WIDTH), F32),
        scratch_shapes=[pltpu.VMEM((tm, D_MODEL), BF16)],
        compiler_params=_cparams(("parallel", "arbitrary")),
        name="inproj",
    )(x2, g, w)


def _mlstm_kernel(qk_ref, v_ref, o_ref, sm_ref, conv_ref, gb_ref, ng_ref, out_ref,
                  xbuf, cst, mst):
    L = CHUNK
    c = pl.program_id(1)

    @pl.when(c == 0)
    def _():
        xbuf[0:SUBLANES, :] = jnp.zeros((SUBLANES, 2 * M_HEADS * M_QK), F32)
        cst[...] = jnp.zeros_like(cst)
        mst[...] = jnp.zeros_like(mst)

    x = qk_ref[...]
    xbuf[SUBLANES:SUBLANES + L, :] = x
    w = conv_ref[...]
    acc = w[CONV_W - 1:CONV_W, :] * x
    for j in range(CONV_W - 1):
        off = SUBLANES - (CONV_W - 1) + j
        acc = acc + w[j:j + 1, :] * xbuf[off:off + L, :]
    xbuf[0:SUBLANES, :] = x[L - SUBLANES:L, :]
    qk = acc * jax.nn.sigmoid(acc)
    kt_all = qk[:, M_HEADS * M_QK:].T

    g = sm_ref[...] + gb_ref[0:1, :]
    lane = lax.broadcasted_iota(I32, (L, LANES), 1)
    xg = jnp.where((lane >= SM_F) & (lane < SM_F + M_HEADS), jax.nn.log_sigmoid(g), g)
    r_i = lax.broadcasted_iota(I32, (L, L), 0)
    c_i = lax.broadcasted_iota(I32, (L, L), 1)
    causal = r_i >= c_i
    bc = jnp.dot(causal.astype(F32), xg, preferred_element_type=F32,
                 precision=lax.Precision.HIGHEST)
    xg_t = xg.T
    bc_t = bc.T
    ones = jnp.ones((L, M_V), BF16)

    for h in range(M_HEADS):
        bc_c = bc[:, SM_F + h:SM_F + h + 1]
        li_r = xg_t[SM_I + h:SM_I + h + 1, :]
        bc_r = bc_t[SM_F + h:SM_F + h + 1, :]
        m_prev = mst[h:h + 1, 0:1]
        b_last = bc_c[L - 1:L, :]
        b_r = li_r - bc_r
        d_log = jnp.where(causal, bc_c + b_r, -jnp.inf)
        inter = bc_c + m_prev
        m_row = jnp.maximum(inter, jnp.max(d_log, axis=-1, keepdims=True))
        w_intra = jnp.exp(d_log - m_row)
        w_inter = jnp.exp(inter - m_row)

        qb = (qk[:, h * M_QK:(h + 1) * M_QK] * (M_QK ** -0.5)).astype(BF16)
        kt = kt_all[h * M_QK:(h + 1) * M_QK, :]
        vb = v_ref[:, h * M_V:(h + 1) * M_V].astype(BF16)
        vaug = jnp.concatenate([vb, ones], axis=1)
        caug = cst[h]
        sc = _dot(qb, kt.astype(BF16)) * w_intra
        num = _dot(sc.astype(BF16), vaug) + w_inter * _dot(qb, caug.astype(BF16))
        den = num[:, M_V:M_V + 1]
        out = num[:, :M_V] / jnp.maximum(jnp.abs(den), jnp.exp(-m_row))

        g_r = b_last + b_r
        m_new = jnp.maximum(b_last + m_prev, jnp.max(g_r, axis=-1, keepdims=True))
        wk_r = jnp.exp(g_r - m_new)
        decay = jnp.exp(b_last + m_prev - m_new)
        cst[h] = decay * caug + _dot((kt * wk_r).astype(BF16), vaug)
        mst[h:h + 1, :] = jnp.broadcast_to(m_new, (1, LANES))

        y = _group_norm(out, ng_ref[0:1, h * M_V:(h + 1) * M_V])
        gate = jax.nn.sigmoid(o_ref[:, h * M_V:(h + 1) * M_V])
        out_ref[:, h * M_V:(h + 1) * M_V] = (y * gate).astype(BF16)


def _mlstm(z, conv_w, gbias, ng, b, s):
    nc = s // CHUNK
    row = lambda bi, ci: bi * nc + ci
    w512 = 2 * M_HEADS * M_QK
    return pl.pallas_call(
        _mlstm_kernel,
        grid=(b, nc),
        in_specs=[pl.BlockSpec((CHUNK, w512), lambda bi, ci: (row(bi, ci), Z_MQK // w512)),
                  pl.BlockSpec((CHUNK, M_WIDTH), lambda bi, ci: (row(bi, ci), Z_MV // M_WIDTH)),
                  pl.BlockSpec((CHUNK, M_WIDTH), lambda bi, ci: (row(bi, ci), Z_MO // M_WIDTH)),
                  pl.BlockSpec((CHUNK, LANES), lambda bi, ci: (row(bi, ci), Z_SMALL // LANES)),
                  _resident((SUBLANES, w512)),
                  _resident((SUBLANES, LANES)),
                  _resident((SUBLANES, M_WIDTH))],
        out_specs=pl.BlockSpec((CHUNK, M_WIDTH), lambda bi, ci: (row(bi, ci), 0)),
        out_shape=jax.ShapeDtypeStruct((b * s, M_WIDTH), BF16),
        scratch_shapes=[pltpu.VMEM((SUBLANES + CHUNK, w512), F32),
                        pltpu.VMEM((M_HEADS, M_QK, 2 * M_V), F32),
                        pltpu.VMEM((SUBLANES, LANES), F32)],
        compiler_params=_cparams(("parallel", "arbitrary")),
        name="mlstm",
    )(z, z, z, z, conv_w, gbias, ng)


def _ret_kernel(q_ref, k_ref, v_ref, g_ref, cos_ref, sin_ref, intra_ref, qdec_ref, kdec_ref,
                cdec_ref, ng_ref, out_ref, st):
    c = pl.program_id(1)

    @pl.when(c == 0)
    def _():
        st[...] = jnp.zeros_like(st)

    cosf = cos_ref[...]
    sinf = sin_ref[...]
    half = R_QK // 2
    for h in range(R_HEADS):
        sl = slice(h * R_QK, (h + 1) * R_QK)
        q = q_ref[:, sl]
        k = k_ref[:, sl]
        qr = (q * cosf + pltpu.roll(q, half, 1) * sinf).astype(BF16)
        kr = (k * cosf + pltpu.roll(k, half, 1) * sinf) * (R_QK ** -0.5)
        kt = kr.T
        vb = v_ref[:, sl].astype(BF16)
        state = st[h]
        sc = _dot(qr, kt.astype(BF16)) * intra_ref[h]
        out = _dot(sc.astype(BF16), vb) + qdec_ref[:, h:h + 1] * _dot(qr, state.astype(BF16))
        st[h] = cdec_ref[h:h + 1, :] * state + _dot((kt * kdec_ref[h:h + 1, :]).astype(BF16), vb)
        y = _group_norm(out, ng_ref[0:1, sl])
        gt = g_ref[:, sl]
        out_ref[:, sl] = (y * (gt * jax.nn.sigmoid(gt))).astype(BF16)


def _retention(z, cosf, sinf, intra, qdec, kdec, cdec, ng, b, s):
    nc = s // CHUNK
    row = lambda bi, ci: bi * nc + ci
    zspec = lambda off: pl.BlockSpec((CHUNK, R_WIDTH), lambda bi, ci: (row(bi, ci), off // R_WIDTH))
    return pl.pallas_call(
        _ret_kernel,
        grid=(b, nc),
        in_specs=[zspec(Z_RQ), zspec(Z_RK), zspec(Z_RV), zspec(Z_RG),
                  pl.BlockSpec((CHUNK, R_QK), lambda bi, ci: (ci, 0)),
                  pl.BlockSpec((CHUNK, R_QK), lambda bi, ci: (ci, 0)),
                  _resident((R_HEADS, CHUNK, CHUNK)),
                  _resident((CHUNK, LANES)),
                  _resident((SUBLANES, CHUNK)),
                  _resident((SUBLANES, R_V)),
                  _resident((SUBLANES, R_WIDTH))],
        out_specs=pl.BlockSpec((CHUNK, R_WIDTH), lambda bi, ci: (row(bi, ci), 0)),
        out_shape=jax.ShapeDtypeStruct((b * s, R_WIDTH), BF16),
        scratch_shapes=[pltpu.VMEM((R_HEADS, R_QK, R_V), F32)],
        compiler_params=_cparams(("parallel", "arbitrary")),
        name="retention",
    )(z, z, z, z, cosf, sinf, intra, qdec, kdec, cdec, ng)


def _dsa_prep_kernel(cq_ref, ckv_ref, sm_ref, qn_ref, kvn_ref, kig_ref, wuq_t_ref, wuqi_t_ref,
                     wuk_t_ref, qlat_ref, qidx_ref, widx_ref, kidx_ref, ckvn_ref, ckvt_ref):
    tm = cq_ref.shape[0]
    c_qt = _rms(cq_ref[...], qn_ref[0:1, :]).T.astype(BF16)
    q_t = _dot(wuq_t_ref[...], c_qt).astype(BF16)
    for h in range(A_HEADS):
        ql_t = (_dot(wuk_t_ref[h], q_t[h * A_HEAD_DIM:(h + 1) * A_HEAD_DIM, :])
                * (A_HEAD_DIM ** -0.5)).astype(BF16)
        qi_t = _dot(wuqi_t_ref[h], c_qt).astype(BF16)
        for j in range(tm // Q_BLOCK):
            cols = slice(j * Q_BLOCK, (j + 1) * Q_BLOCK)
            qlat_ref[j, :, h * Q_BLOCK:(h + 1) * Q_BLOCK] = ql_t[:, cols]
            qidx_ref[j, :, h * Q_BLOCK:(h + 1) * Q_BLOCK] = qi_t[:, cols]
    sm = sm_ref[...]
    w_t = sm.T[SM_W:SM_W + IDX_HEADS, :] * (IDX_HEADS ** -0.5 * IDX_DIM ** -0.5)
    for j in range(tm // Q_BLOCK):
        widx_ref[j] = w_t[:, j * Q_BLOCK:(j + 1) * Q_BLOCK]
    lane = lax.broadcasted_iota(I32, sm.shape, 1)
    inside = lane < IDX_DIM
    mu = jnp.sum(jnp.where(inside, sm, 0.0), axis=-1, keepdims=True) * (1.0 / IDX_DIM)
    xc = jnp.where(inside, sm - mu, 0.0)
    var = jnp.sum(xc * xc, axis=-1, keepdims=True) * (1.0 / IDX_DIM)
    kidx_ref[...] = (xc * lax.rsqrt(var + EPS) * kig_ref[0:1, :]).astype(BF16)
    ckvn = _rms(ckv_ref[...], kvn_ref[0:1, :])
    ckvn_ref[...] = ckvn.astype(BF16)
    ckvn_t = ckvn.T.astype(BF16)
    for j in range(tm // KEY_TILE):
        ckvt_ref[j] = ckvn_t[:, j * KEY_TILE:(j + 1) * KEY_TILE]


def _dsa_prep(z, qn, kvn, kig, wuq_t, wuqi_t, wuk_t, tm):
    t = z.shape[0]
    hq = A_HEADS * Q_BLOCK
    nqb = tm // Q_BLOCK
    qspec = pl.BlockSpec((nqb, LANES, hq), lambda i: (i, 0, 0))
    return pl.pallas_call(
        _dsa_prep_kernel,
        grid=(t // tm,),
        in_specs=[pl.BlockSpec((tm, Q_RANK), lambda i: (i, Z_ACQ // Q_RANK)),
                  pl.BlockSpec((tm, KV_RANK), lambda i: (i, Z_ACKV // KV_RANK)),
                  pl.BlockSpec((tm, LANES), lambda i: (i, Z_SMALL // LANES)),
                  _resident((SUBLANES, Q_RANK)),
                  _resident((SUBLANES, KV_RANK)),
                  _resident((SUBLANES, LANES)),
                  _resident((A_WIDTH, Q_RANK)),
                  _resident((IDX_HEADS, LANES, Q_RANK)),
                  _resident((A_HEADS, KV_RANK, A_HEAD_DIM))],
        out_specs=[qspec, qspec,
                   pl.BlockSpec((nqb, IDX_HEADS, Q_BLOCK), lambda i: (i, 0, 0)),
                   pl.BlockSpec((tm, LANES), lambda i: (i, 0)),
                   pl.BlockSpec((tm, KV_RANK), lambda i: (i, 0)),
                   pl.BlockSpec((tm // KEY_TILE, KV_RANK, KEY_TILE), lambda i: (i, 0, 0))],
        out_shape=[jax.ShapeDtypeStruct((t // Q_BLOCK, KV_RANK, hq), BF16),
                   jax.ShapeDtypeStruct((t // Q_BLOCK, LANES, hq), BF16),
                   jax.ShapeDtypeStruct((t // Q_BLOCK, IDX_HEADS, Q_BLOCK), F32),
                   jax.ShapeDtypeStruct((t, LANES), BF16),
                   jax.ShapeDtypeStruct((t, KV_RANK), BF16),
                   jax.ShapeDtypeStruct((t // KEY_TILE, KV_RANK, KEY_TILE), BF16)],
        compiler_params=_cparams(("parallel",)),
        name="dsa_prep",
    )(z, z, z, qn, kvn, kig, wuq_t, wuqi_t, wuk_t)


def _fold(x, op):
    parts = [x[i:i + SUBLANES] for i in range(0, x.shape[0], SUBLANES)]
    while len(parts) > 1:
        parts = [op(parts[i], parts[i + 1]) for i in range(0, len(parts), 2)]
    return parts[0]


def _dsa_kernel(qidx_ref, qlat_ref, widx_ref, kidx_ref, ckv_ref, ckvt_ref, bias_ref, wuv_t_ref,
                out_ref, keys_scr, lg_scr, acc_scr, tie_scr, *, top_k, n_keys):
    QB = Q_BLOCK
    KT = KEY_TILE
    qblk = pl.program_id(1)
    nk = (qblk * QB + QB + KT - 1) // KT
    kf = float(top_k)
    hcols = [slice(h * QB, (h + 1) * QB) for h in range(A_HEADS)]
    hrows = [slice(h * SUBLANES, (h + 1) * SUBLANES) for h in range(A_HEADS)]

    q_pos = qblk * QB + lax.broadcasted_iota(I32, (KT, QB), 1)
    key_off = lax.broadcasted_iota(I32, (KT, QB), 0)

    qi_t = qidx_ref[0]
    w_rows = widx_ref[0]

    def score_body(kt, carry):
        s_t = _dot(kidx_ref[0, kt], qi_t)
        tot = jnp.zeros((KT, QB), F32)
        for g in range(IDX_HEADS):
            tot = tot + w_rows[g:g + 1, :] * jnp.maximum(s_t[:, hcols[g]], 0.0)
        key_pos = kt * KT + key_off
        tot = jnp.where(key_pos <= q_pos, tot + 0.0, -jnp.inf)
        bits = pltpu.bitcast(tot, I32)
        keys_scr[kt] = bits ^ ((bits >> 31) & 0x7FFFFFFF)
        return carry

    lax.fori_loop(0, nk, score_body, 0)

    def count(pred):
        def body(kt, acc):
            return acc + _fold(jnp.where(pred(keys_scr[kt], kt), 1.0, 0.0), jnp.add)
        acc = lax.fori_loop(0, nk, body, jnp.zeros((SUBLANES, QB), F32))
        return jnp.sum(acc, axis=0, keepdims=True)

    def count_ge(cand):
        return count(lambda keys, kt: keys >= cand)

    thr = jnp.where(count_ge(jnp.zeros((1, QB), I32)) >= kf, 0, INT_MIN).astype(I32)

    def bit_body(i, thr):
        cand = thr | (jnp.int32(1) << (30 - i))
        return jnp.where(count_ge(cand) >= kf, cand, thr)

    thr = lax.fori_loop(0, 31, bit_body, thr)

    tie_scr[...] = jnp.full((SUBLANES, QB), n_keys, I32)
    over = jnp.max(jnp.where(count_ge(thr) > kf, 1.0, 0.0))

    @pl.when(over > 0.0)
    def _():
        need = kf - count(lambda keys, kt: keys > thr)

        def idx_body(i, p):
            cand = p | (jnp.int32(1) << (n_keys.bit_length() - 1 - i))
            cnt = count(lambda keys, kt: (keys == thr) & (kt * KT + key_off < cand))
            return jnp.where(cnt < need, cand, p)

        p = lax.fori_loop(0, n_keys.bit_length(), idx_body, jnp.zeros((1, QB), I32))
        tie_scr[...] = jnp.broadcast_to(p, (SUBLANES, QB))

    tie_hi = tie_scr[0:1, :]

    ql_t = qlat_ref[0]

    def pass_a(kt, mx):
        lg_t = _dot(ckv_ref[0, kt], ql_t)
        keys = keys_scr[kt]
        key_pos = kt * KT + key_off
        sel = (keys > thr) | ((keys == thr) & (key_pos <= tie_hi))
        sel = sel & (key_pos <= q_pos)
        didx = jnp.minimum(qblk - (KT // QB) * kt, 3)
        new = []
        for h in range(A_HEADS):
            lh = jnp.where(sel, lg_t[:, hcols[h]] + bias_ref[didx, h], -jnp.inf)
            lg_scr[kt, :, hcols[h]] = lh
            new.append(jnp.maximum(mx[hrows[h]], _fold(lh, jnp.maximum)))
        return jnp.concatenate(new, axis=0)

    mx = lax.fori_loop(0, nk, pass_a, jnp.full((A_HEADS * SUBLANES, QB), -jnp.inf, F32))
    m_rows = [jnp.max(mx[hrows[h]], axis=0, keepdims=True) for h in range(A_HEADS)]

    acc_scr[...] = jnp.zeros_like(acc_scr)

    def pass_b(kt, l):
        lg = lg_scr[kt]
        ps, new = [], []
        for h in range(A_HEADS):
            p = jnp.exp(lg[:, hcols[h]] - m_rows[h])
            new.append(l[hrows[h]] + _fold(p, jnp.add))
            ps.append(p.astype(BF16))
        acc_scr[...] = acc_scr[...] + _dot(ckvt_ref[0, kt], jnp.concatenate(ps, axis=1))
        return jnp.concatenate(new, axis=0)

    l = lax.fori_loop(0, nk, pass_b, jnp.zeros((A_HEADS * SUBLANES, QB), F32))

    outs = []
    for h in range(A_HEADS):
        o_t = (acc_scr[:, hcols[h]] / jnp.sum(l[hrows[h]], axis=0, keepdims=True)).astype(BF16)
        outs.append(_dot(wuv_t_ref[h], o_t))
    out_ref[...] = jnp.concatenate(outs, axis=0).T.astype(BF16)


def _dsa(qidx_t, qlat_t, widx, kidx, ckv, ckv_t, bias_tab, wuv_t, b, s):
    nq = s // Q_BLOCK
    nt = s // KEY_TILE
    top_k = min(TOPK_MAX, s // 4)
    hq = A_HEADS * Q_BLOCK
    row = lambda bi, qi: bi * nq + qi
    qspec = pl.BlockSpec((1, LANES, hq), lambda bi, qi: (row(bi, qi), 0, 0))
    kspec = pl.BlockSpec((1, nt, KEY_TILE, LANES), lambda bi, qi: (bi, 0, 0, 0))
    return pl.pallas_call(
        functools.partial(_dsa_kernel, top_k=top_k, n_keys=s),
        grid=(b, nq),
        in_specs=[qspec, qspec,
                  pl.BlockSpec((1, IDX_HEADS, Q_BLOCK), lambda bi, qi: (row(bi, qi), 0, 0)),
                  kspec, kspec,
                  pl.BlockSpec((1, nt, KV_RANK, KEY_TILE), lambda bi, qi: (bi, 0, 0, 0)),
                  _resident((4, A_HEADS, KEY_TILE, Q_BLOCK)),
                  _resident((A_HEADS, A_HEAD_DIM, KV_RANK))],
        out_specs=pl.BlockSpec((Q_BLOCK, A_WIDTH), lambda bi, qi: (row(bi, qi), 0)),
        out_shape=jax.ShapeDtypeStruct((b * s, A_WIDTH), BF16),
        scratch_shapes=[pltpu.VMEM((nt, KEY_TILE, Q_BLOCK), I32),
                        pltpu.VMEM((nt, KEY_TILE, hq), F32),
                        pltpu.VMEM((KV_RANK, hq), F32),
                        pltpu.VMEM((SUBLANES, Q_BLOCK), I32)],
        compiler_params=_cparams(("parallel", "arbitrary")),
        name="dsa",
    )(qidx_t, qlat_t, widx, kidx.reshape(b, nt, KEY_TILE, LANES), ckv.reshape(b, nt, KEY_TILE, LANES),
      ckv_t.reshape(b, nt, KV_RANK, KEY_TILE), bias_tab, wuv_t)


FF_CHUNK = 256


def _merge_ffn_kernel(x_ref, hm_ref, ha_ref, hr_ref, gm_ref, ga_ref, gr_ref, pm_ref, pa_ref,
                      pr_ref, wo_ref, n2_ref, wg_ref, wu_ref, wd_ref, fg_ref, out_ref, *, final):
    y = (jax.nn.sigmoid(gm_ref[...]) * _dot(hm_ref[...], pm_ref[...])
         + jax.nn.sigmoid(ga_ref[...]) * _dot(ha_ref[...], pa_ref[...])
         + jax.nn.sigmoid(gr_ref[...]) * _dot(hr_ref[...], pr_ref[...]))
    x1 = x_ref[...] + _dot(y.astype(BF16), wo_ref[...])
    h2 = _rms(x1, n2_ref[0:1, :]).astype(BF16)
    acc = x1
    for c in range(D_FF // FF_CHUNK):
        cs = slice(c * FF_CHUNK, (c + 1) * FF_CHUNK)
        gate = _dot(h2, wg_ref[:, cs])
        up = _dot(h2, wu_ref[:, cs])
        act = (gate * jax.nn.sigmoid(gate) * up).astype(BF16)
        acc = acc + _dot(act, wd_ref[cs, :])
    if final:
        acc = _rms(acc, fg_ref[0:1, :])
    out_ref[...] = acc


def _merge_ffn(x2, hm, ha, hr, z, pm, pa, pr, wo, n2, wg, wu, wd, fg, tm, final):
    t = x2.shape[0]
    rowspec = lambda w, cb=0: pl.BlockSpec((tm, w), lambda i: (i, cb))
    return pl.pallas_call(
        functools.partial(_merge_ffn_kernel, final=final),
        grid=(t // tm,),
        in_specs=[rowspec(D_MODEL), rowspec(M_WIDTH), rowspec(A_WIDTH), rowspec(R_WIDTH),
                  rowspec(D_MODEL, Z_GM // D_MODEL), rowspec(D_MODEL, Z_GA // D_MODEL),
                  rowspec(D_MODEL, Z_GR // D_MODEL),
                  _resident((M_WIDTH, D_MODEL)), _resident((A_WIDTH, D_MODEL)),
                  _resident((R_WIDTH, D_MODEL)), _resident((D_MODEL, D_MODEL)),
                  _resident((SUBLANES, D_MODEL)),
                  _resident((D_MODEL, D_FF)), _resident((D_MODEL, D_FF)),
                  _resident((D_FF, D_MODEL)), _resident((SUBLANES, D_MODEL))],
        out_specs=rowspec(D_MODEL),
        out_shape=jax.ShapeDtypeStruct((t, D_MODEL), F32),
        compiler_params=_cparams(("parallel",)),
        name="merge_ffn",
    )(x2, hm, ha, hr, z, z, z, pm, pa, pr, wo, n2, wg, wu, wd, fg)


def _pad_rows(v, rows=SUBLANES):
    v = jnp.atleast_2d(v).astype(F32)
    return jnp.pad(v, ((0, rows - v.shape[0]), (0, 0)))


def _t5_bucket(dist):
    max_exact = N_BUCKETS // 2
    d_f = jnp.maximum(dist, 1).astype(F32)
    large = max_exact + (jnp.log(d_f / max_exact) / math.log(MAX_DISTANCE / max_exact)
                         * (N_BUCKETS - max_exact)).astype(I32)
    large = jnp.minimum(large, N_BUCKETS - 1)
    return jnp.where(dist < max_exact, dist, large)


def _bias_table(rel_bias):
    d = jnp.arange(4, dtype=I32)[:, None, None]
    j = jnp.arange(KEY_TILE, dtype=I32)[None, :, None]
    i = jnp.arange(Q_BLOCK, dtype=I32)[None, None, :]
    bucket = _t5_bucket(jnp.maximum(i + d * Q_BLOCK - j, 0))
    onehot = (bucket[None] == jnp.arange(N_BUCKETS, dtype=I32)[:, None, None, None]).astype(F32)
    return jnp.einsum("nh,ndkq->dhkq", rel_bias.astype(F32), onehot, precision=lax.Precision.HIGHEST)


def _rotary_tables(s):
    half = R_QK // 2
    freqs = ROPE_BASE ** (-jnp.linspace(0.0, 1.0, half, dtype=F32))
    ang = jnp.arange(s, dtype=F32)[:, None] * freqs[None, :]
    cos, sin = jnp.cos(ang), jnp.sin(ang)
    return jnp.concatenate([cos, cos], axis=1), jnp.concatenate([-sin, sin], axis=1)


def _retention_tables():
    h = R_HEADS
    log_gamma = jnp.log1p(-jnp.exp2(-5.0 - jnp.arange(h, dtype=F32)))
    pos = jnp.arange(CHUNK, dtype=F32)
    diff = pos[:, None] - pos[None, :]
    intra = jnp.where(diff >= 0, jnp.exp(jnp.maximum(diff, 0.0)[None] * log_gamma[:, None, None]), 0.0)
    q_decay = jnp.exp((pos[None, :] + 1.0) * log_gamma[:, None])
    k_decay = jnp.exp((CHUNK - 1.0 - pos[None, :]) * log_gamma[:, None])
    chunk_decay = jnp.exp(CHUNK * log_gamma)
    qdec = jnp.pad(q_decay.T, ((0, 0), (0, LANES - h)))
    kdec = _pad_rows(k_decay)
    cdec = _pad_rows(jnp.broadcast_to(chunk_decay[:, None], (h, R_V)))
    return intra, qdec, kdec, cdec


def _layout_w_in(w_in):
    parts = jnp.split(w_in, [int(c) for c in np.cumsum(IN_SPLITS)[:-1]], axis=-1)
    (m_q, m_k, m_v, m_i, m_f, m_o, a_cq, a_kidx, a_widx, a_ckv,
     r_q, r_k, r_v, r_g, g_m, g_a, g_r) = parts
    small = jnp.concatenate([a_kidx, m_i, m_f, a_widx], axis=-1)
    small = jnp.pad(small, ((0, 0), (0, LANES - small.shape[-1])))
    w = jnp.concatenate([m_q, m_k, m_v, m_o, a_cq, a_ckv, small, r_q, r_k, r_v, r_g,
                         g_m, g_a, g_r], axis=-1)
    assert w.shape[-1] == Z_WIDTH
    return w.astype(BF16)


def _row_tile(t, want):
    while t % want:
        want //= 2
    return want


def kernel(x, rel_bias, final_norm_g, norm1_g, w_in, m_conv, m_ibias, m_fbias, m_norm_g, a_qnorm_g, a_wuq, a_wuq_idx, a_kidx_g, a_kvnorm_g, a_wuk, a_wuv, r_norm_g, p_m, p_a, p_r, w_out, norm2_g, w_gate, w_up, w_down):
    b, s, d = x.shape
    depth = w_in.shape[0]
    assert d == D_MODEL and s % KEY_TILE == 0
    t = b * s
    tm_in = _row_tile(t, 1024)
    tm = _row_tile(t, 512)

    bias_tab = _bias_table(rel_bias)
    cosf, sinf = _rotary_tables(s)
    intra, qdec, kdec, cdec = _retention_tables()
    fg = _pad_rows(final_norm_g)

    x2 = x.reshape(t, d)
    for l in range(depth):
        z = _inproj(x2, norm1_g[l][None, :], _layout_w_in(w_in[l]), tm_in, 1024)

        gbias = jnp.zeros((LANES,), F32)
        gbias = gbias.at[SM_I:SM_I + M_HEADS].set(m_ibias[l]).at[SM_F:SM_F + M_HEADS].set(m_fbias[l])
        hm = _mlstm(z, _pad_rows(m_conv[l]), _pad_rows(gbias), _pad_rows(m_norm_g[l]), b, s)

        hr = _retention(z, cosf, sinf, intra, qdec, kdec, cdec, _pad_rows(r_norm_g[l]), b, s)

        wuqi_t = a_wuq_idx[l].reshape(Q_RANK, IDX_HEADS, IDX_DIM).transpose(1, 2, 0)
        wuqi_t = jnp.pad(wuqi_t, ((0, 0), (0, LANES - IDX_DIM), (0, 0))).astype(BF16)
        kig = jnp.pad(a_kidx_g[l], (0, LANES - IDX_DIM))
        qlat_t, qidx_t, widx, kidx, ckv, ckv_t = _dsa_prep(
            z, _pad_rows(a_qnorm_g[l]), _pad_rows(a_kvnorm_g[l]), _pad_rows(kig),
            a_wuq[l].T.astype(BF16), wuqi_t, jnp.swapaxes(a_wuk[l], 1, 2).astype(BF16), tm)
        ha = _dsa(qidx_t, qlat_t, widx, kidx, ckv, ckv_t, bias_tab,
                  jnp.swapaxes(a_wuv[l], 1, 2).astype(BF16), b, s)

        x2 = _merge_ffn(x2, hm, ha, hr, z, p_m[l].astype(BF16), p_a[l].astype(BF16),
                        p_r[l].astype(BF16), w_out[l].astype(BF16), _pad_rows(norm2_g[l]),
                        w_gate[l].astype(BF16), w_up[l].astype(BF16), w_down[l].astype(BF16),
                        fg, tm, final=(l == depth - 1))
    return x2.reshape(b, s, d)
```

```python
import functools
import math

import numpy as np
import jax
import jax.numpy as jnp
from jax import lax
from jax.experimental import pallas as pl
from jax.experimental.pallas import tpu as pltpu

F32 = jnp.float32
BF16 = jnp.bfloat16
I32 = jnp.int32

D_MODEL = 1024
M_HEADS, M_QK, M_V = 4, 64, 128
M_WIDTH = M_HEADS * M_V
CONV_W = 4
A_HEADS, A_HEAD_DIM = 8, 64
A_WIDTH = A_HEADS * A_HEAD_DIM
Q_RANK, KV_RANK = 256, 128
IDX_HEADS, IDX_DIM = 8, 64
TOPK_MAX = 256
R_HEADS, R_QK, R_V = 4, 128, 128
R_WIDTH = R_HEADS * R_V
CHUNK = 128
Q_BLOCK = 128
N_BUCKETS = 32
MAX_DISTANCE = 128
ROPE_BASE = 10000.0
EPS = 1e-6
D_FF = -(((-8 * D_MODEL) // 3) // 256) * 256
IN_SPLITS = (M_HEADS * M_QK, M_HEADS * M_QK, M_WIDTH, M_HEADS, M_HEADS, M_WIDTH,
             Q_RANK, IDX_DIM, IDX_HEADS, KV_RANK,
             R_HEADS * R_QK, R_HEADS * R_QK, R_WIDTH, R_WIDTH,
             D_MODEL, D_MODEL, D_MODEL)

LANES = 128
SUBLANES = 8
VMEM_LIMIT = 56 * 1024 * 1024

F_MQK = 0
F_ACQ = 512
F_ACKV = 768
F_SMALL = 896
F_RQ, F_RK = 1024, 1536
ZF_WIDTH = 2048
B_MV, B_MO, B_RV, B_RG = 0, 512, 1024, 1536
B_GM, B_GA, B_GR = 2048, 3072, 4096
ZB_WIDTH = 5120
SM_KIDX, SM_I, SM_F, SM_W = 0, 64, 68, 72

KEY_TILE = 256
SCAN_ROWS = 2
INT_MIN = -2 ** 31


def _cparams(sem):
    return pltpu.CompilerParams(dimension_semantics=sem, vmem_limit_bytes=VMEM_LIMIT)


def _resident(shape):
    nd = len(shape)
    return pl.BlockSpec(shape, lambda *_: (0,) * nd, pipeline_mode=pl.Buffered(1))


def _rms(x, g):
    return x * lax.rsqrt(jnp.mean(x * x, axis=-1, keepdims=True) + EPS) * g


def _group_norm(x, g):
    mu = jnp.mean(x, axis=-1, keepdims=True)
    xc = x - mu
    return xc * lax.rsqrt(jnp.mean(xc * xc, axis=-1, keepdims=True) + EPS) * g


def _dot(a, b):
    return jnp.dot(a, b, preferred_element_type=F32)


def _inproj_kernel(x_ref, g_ref, w_ref, zf_ref, zb_ref, h_scr, *, nf):
    j = pl.program_id(1)

    @pl.when(j == 0)
    def _():
        h_scr[...] = _rms(x_ref[...], g_ref[...]).astype(BF16)

    @pl.when(j < nf)
    def _():
        zf_ref[...] = _dot(h_scr[...], w_ref[...])

    @pl.when(j >= nf)
    def _():
        zb_ref[...] = _dot(h_scr[...], w_ref[...]).astype(BF16)


def _inproj(x2, g, w, tm, tn):
    t = x2.shape[0]
    nf = ZF_WIDTH // tn
    return pl.pallas_call(
        functools.partial(_inproj_kernel, nf=nf),
        grid=(t // tm, (ZF_WIDTH + ZB_WIDTH) // tn),
        in_specs=[pl.BlockSpec((tm, D_MODEL), lambda i, j: (i, 0)),
                  pl.BlockSpec((1, D_MODEL), lambda i, j: (0, 0)),
                  pl.BlockSpec((D_MODEL, tn), lambda i, j: (0, j))],
        out_specs=[pl.BlockSpec((tm, tn), lambda i, j: (i, jnp.minimum(j, nf - 1))),
                   pl.BlockSpec((tm, tn), lambda i, j: (i, jnp.maximum(j - nf, 0)))],
        out_shape=[jax.ShapeDtypeStruct((t, ZF_WIDTH), F32),
                   jax.ShapeDtypeStruct((t, ZB_WIDTH), BF16)],
        scratch_shapes=[pltpu.VMEM((tm, D_MODEL), BF16)],
        compiler_params=_cparams(("parallel", "arbitrary")),
        name="inproj",
    )(x2, g, w)


def _mlstm_kernel(qk_ref, v_ref, o_ref, sm_ref, conv_ref, gb_ref, ng_ref, out_ref,
                  xbuf, cst, mst):
    for r in range(SCAN_ROWS):
        _mlstm_row(qk_ref.at[r, 0], v_ref.at[r, 0], o_ref.at[r, 0], sm_ref.at[r, 0], conv_ref,
                   gb_ref, ng_ref, out_ref.at[r, 0], xbuf.at[r], cst.at[r], mst.at[r])


def _mlstm_row(qk_ref, v_ref, o_ref, sm_ref, conv_ref, gb_ref, ng_ref, out_ref, xbuf, cst, mst):
    L = CHUNK
    c = pl.program_id(1)

    @pl.when(c == 0)
    def _():
        xbuf[0:SUBLANES, :] = jnp.zeros((SUBLANES, 2 * M_HEADS * M_QK), F32)
        cst[...] = jnp.zeros_like(cst)
        mst[...] = jnp.zeros_like(mst)

    x = qk_ref[...]
    xbuf[SUBLANES:SUBLANES + L, :] = x
    w = conv_ref[...]
    acc = w[CONV_W - 1:CONV_W, :] * x
    for j in range(CONV_W - 1):
        off = SUBLANES - (CONV_W - 1) + j
        acc = acc + w[j:j + 1, :] * xbuf[off:off + L, :]
    xbuf[0:SUBLANES, :] = x[L - SUBLANES:L, :]
    qk = acc * jax.nn.sigmoid(acc)
    kt_all = qk[:, M_HEADS * M_QK:].T

    g = sm_ref[...] + gb_ref[0:1, :]
    lane = lax.broadcasted_iota(I32, (L, LANES), 1)
    xg = jnp.where((lane >= SM_F) & (lane < SM_F + M_HEADS), jax.nn.log_sigmoid(g), g)
    r_i = lax.broadcasted_iota(I32, (L, L), 0)
    c_i = lax.broadcasted_iota(I32, (L, L), 1)
    causal = r_i >= c_i
    exact = dict(preferred_element_type=F32, precision=lax.Precision.HIGHEST)
    rows_if = xg.T[SM_I:SM_I + 2 * M_HEADS, :]
    cum_r = jnp.dot(rows_if, (r_i <= c_i).astype(F32), **exact)
    bc = jnp.dot(causal.astype(F32), xg, **exact)
    ones = jnp.ones((L, M_V), BF16)

    c_all = cst[...]
    m_all = mst[...]
    new_c, new_m = [], []
    for h in range(M_HEADS):
        bc_q = jnp.broadcast_to(bc[:, SM_F + h:SM_F + h + 1], (L, LANES))
        b_r = rows_if[h:h + 1, :] - cum_r[M_HEADS + h:M_HEADS + h + 1, :]
        m_prev = m_all[h:h + 1, :]
        b_last = bc_q[L - 1:L, :]
        d_log = jnp.where(causal, bc_q + b_r, -jnp.inf)
        inter = bc_q + m_prev
        m_row = jnp.maximum(inter, jnp.max(d_log, axis=-1, keepdims=True))
        w_intra = jnp.exp(d_log - m_row)
        w_inter = jnp.exp(inter - m_row)

        qb = (qk[:, h * M_QK:(h + 1) * M_QK] * (M_QK ** -0.5)).astype(BF16)
        kt = kt_all[h * M_QK:(h + 1) * M_QK, :]
        vb = v_ref[:, h * M_V:(h + 1) * M_V]
        vaug = jnp.concatenate([vb, ones], axis=1)
        caug = c_all[h]
        sc = _dot(qb, kt.astype(BF16)) * w_intra
        num = (_dot(sc.astype(BF16), vaug)
               + jnp.concatenate([w_inter, w_inter], axis=1) * _dot(qb, caug.astype(BF16)))
        den = num[:, M_V:]
        out = num[:, :M_V] / jnp.maximum(jnp.abs(den), jnp.exp(-m_row))

        g_r = b_last + b_r
        m_new = jnp.maximum(b_last + m_prev, jnp.max(g_r, axis=-1, keepdims=True))
        wk_r = jnp.exp(g_r - m_new)
        decay = jnp.exp(b_last + m_prev - m_new)
        new_c.append(jnp.concatenate([decay, decay], axis=1) * caug
                     + _dot((kt * wk_r).astype(BF16), vaug))
        new_m.append(m_new)

        y = _group_norm(out, ng_ref[0:1, h * M_V:(h + 1) * M_V])
        gate = jax.nn.sigmoid(o_ref[:, h * M_V:(h + 1) * M_V].astype(F32))
        out_ref[:, h * M_V:(h + 1) * M_V] = (y * gate).astype(BF16)

    for h in range(M_HEADS):
        cst[h] = new_c[h]
    mst[0:M_HEADS, :] = jnp.concatenate(new_m, axis=0)


def _mlstm(zf, zb, conv_w, gbias, ng, b, s):
    nc = s // CHUNK
    w512 = 2 * M_HEADS * M_QK
    zf4 = zf.reshape(b, nc, CHUNK, ZF_WIDTH)
    zb4 = zb.reshape(b, nc, CHUNK, ZB_WIDTH)
    zspec = lambda w, off: pl.BlockSpec((SCAN_ROWS, 1, CHUNK, w), lambda bi, ci: (bi, ci, 0, off // w))
    out = pl.pallas_call(
        _mlstm_kernel,
        grid=(b // SCAN_ROWS, nc),
        in_specs=[zspec(w512, F_MQK), zspec(M_WIDTH, B_MV), zspec(M_WIDTH, B_MO), zspec(LANES, F_SMALL),
                  _resident((SUBLANES, w512)),
                  _resident((SUBLANES, LANES)),
                  _resident((SUBLANES, M_WIDTH))],
        out_specs=pl.BlockSpec((SCAN_ROWS, 1, CHUNK, M_WIDTH), lambda bi, ci: (bi, ci, 0, 0)),
        out_shape=jax.ShapeDtypeStruct((b, nc, CHUNK, M_WIDTH), BF16),
        scratch_shapes=[pltpu.VMEM((SCAN_ROWS, SUBLANES + CHUNK, w512), F32),
                        pltpu.VMEM((SCAN_ROWS, M_HEADS, M_QK, 2 * M_V), F32),
                        pltpu.VMEM((SCAN_ROWS, SUBLANES, LANES), F32)],
        compiler_params=_cparams(("parallel", "arbitrary")),
        name="mlstm",
    )(zf4, zb4, zb4, zf4, conv_w, gbias, ng)
    return out.reshape(b * s, M_WIDTH)


def _ret_kernel(q_ref, k_ref, v_ref, g_ref, cos_ref, sin_ref, intra_ref, qdec_ref, kdec_ref,
                cdec_ref, ng_ref, out_ref, st):
    for r in range(SCAN_ROWS):
        _ret_row(q_ref.at[r, 0], k_ref.at[r, 0], v_ref.at[r, 0], g_ref.at[r, 0], cos_ref, sin_ref,
                 intra_ref, qdec_ref, kdec_ref, cdec_ref, ng_ref, out_ref.at[r, 0], st.at[r])


def _ret_row(q_ref, k_ref, v_ref, g_ref, cos_ref, sin_ref, intra_ref, qdec_ref, kdec_ref,
             cdec_ref, ng_ref, out_ref, st):
    c = pl.program_id(1)

    @pl.when(c == 0)
    def _():
        st[...] = jnp.zeros_like(st)

    cosf = cos_ref[...]
    sinf = sin_ref[...]
    half = R_QK // 2
    st_all = st[...]
    new_st = []
    for h in range(R_HEADS):
        sl = slice(h * R_QK, (h + 1) * R_QK)
        q = q_ref[:, sl]
        k = k_ref[:, sl]
        qr = (q * cosf + pltpu.roll(q, half, 1) * sinf).astype(BF16)
        kr = (k * cosf + pltpu.roll(k, half, 1) * sinf) * (R_QK ** -0.5)
        kt = kr.T
        vb = v_ref[:, sl]
        state = st_all[h]
        sc = _dot(qr, kt.astype(BF16)) * intra_ref[h]
        out = _dot(sc.astype(BF16), vb) + qdec_ref[h] * _dot(qr, state.astype(BF16))
        new_st.append(cdec_ref[h:h + 1, :] * state
                      + _dot((kt * kdec_ref[h:h + 1, :]).astype(BF16), vb))
        y = _group_norm(out, ng_ref[0:1, sl])
        gt = g_ref[:, sl].astype(F32)
        out_ref[:, sl] = (y * (gt * jax.nn.sigmoid(gt))).astype(BF16)
    for h in range(R_HEADS):
        st[h] = new_st[h]


def _retention(zf, zb, cosf, sinf, intra, qdec, kdec, cdec, ng, b, s):
    nc = s // CHUNK
    zf4 = zf.reshape(b, nc, CHUNK, ZF_WIDTH)
    zb4 = zb.reshape(b, nc, CHUNK, ZB_WIDTH)
    zspec = lambda off: pl.BlockSpec((SCAN_ROWS, 1, CHUNK, R_WIDTH),
                                     lambda bi, ci: (bi, ci, 0, off // R_WIDTH))
    out = pl.pallas_call(
        _ret_kernel,
        grid=(b // SCAN_ROWS, nc),
        in_specs=[zspec(F_RQ), zspec(F_RK), zspec(B_RV), zspec(B_RG),
                  pl.BlockSpec((CHUNK, R_QK), lambda bi, ci: (ci, 0)),
                  pl.BlockSpec((CHUNK, R_QK), lambda bi, ci: (ci, 0)),
                  _resident((R_HEADS, CHUNK, CHUNK)),
                  _resident((R_HEADS, CHUNK, R_V)),
                  _resident((SUBLANES, CHUNK)),
                  _resident((SUBLANES, R_V)),
                  _resident((SUBLANES, R_WIDTH))],
        out_specs=pl.BlockSpec((SCAN_ROWS, 1, CHUNK, R_WIDTH), lambda bi, ci: (bi, ci, 0, 0)),
        out_shape=jax.ShapeDtypeStruct((b, nc, CHUNK, R_WIDTH), BF16),
        scratch_shapes=[pltpu.VMEM((SCAN_ROWS, R_HEADS, R_QK, R_V), F32)],
        compiler_params=_cparams(("parallel", "arbitrary")),
        name="retention",
    )(zf4, zf4, zb4, zb4, cosf, sinf, intra, qdec, kdec, cdec, ng)
    return out.reshape(b * s, R_WIDTH)


def _dsa_prep_kernel(cq_ref, ckv_ref, sm_ref, qn_ref, kvn_ref, kig_ref, wuq_t_ref, wuqi_t_ref,
                     wuk_t_ref, qlat_ref, qidx_ref, widx_ref, kidx_ref, ckvn_ref, ckvt_ref):
    tm = cq_ref.shape[0]
    c_qt = _rms(cq_ref[...], qn_ref[0:1, :]).T.astype(BF16)
    q_t = _dot(wuq_t_ref[...], c_qt).astype(BF16)
    for h in range(A_HEADS):
        ql_t = (_dot(wuk_t_ref[h], q_t[h * A_HEAD_DIM:(h + 1) * A_HEAD_DIM, :])
                * (A_HEAD_DIM ** -0.5)).astype(BF16)
        qi_t = _dot(wuqi_t_ref[h], c_qt).astype(BF16)
        for j in range(tm // Q_BLOCK):
            cols = slice(j * Q_BLOCK, (j + 1) * Q_BLOCK)
            qlat_ref[j, :, h * Q_BLOCK:(h + 1) * Q_BLOCK] = ql_t[:, cols]
            qidx_ref[j, :, h * Q_BLOCK:(h + 1) * Q_BLOCK] = qi_t[:, cols]
    sm = sm_ref[...]
    w_t = sm.T[SM_W:SM_W + IDX_HEADS, :] * (IDX_HEADS ** -0.5 * IDX_DIM ** -0.5)
    for j in range(tm // Q_BLOCK):
        widx_ref[j] = w_t[:, j * Q_BLOCK:(j + 1) * Q_BLOCK]
    lane = lax.broadcasted_iota(I32, sm.shape, 1)
    inside = lane < IDX_DIM
    mu = jnp.sum(jnp.where(inside, sm, 0.0), axis=-1, keepdims=True) * (1.0 / IDX_DIM)
    xc = jnp.where(inside, sm - mu, 0.0)
    var = jnp.sum(xc * xc, axis=-1, keepdims=True) * (1.0 / IDX_DIM)
    kidx_ref[...] = (xc * lax.rsqrt(var + EPS) * kig_ref[0:1, :]).astype(BF16)
    ckvn = _rms(ckv_ref[...], kvn_ref[0:1, :])
    ckvn_ref[...] = ckvn.astype(BF16)
    ckvn_t = ckvn.T.astype(BF16)
    for j in range(tm // KEY_TILE):
        ckvt_ref[j] = ckvn_t[:, j * KEY_TILE:(j + 1) * KEY_TILE]


def _dsa_prep(z, qn, kvn, kig, wuq_t, wuqi_t, wuk_t, tm):
    t = z.shape[0]
    hq = A_HEADS * Q_BLOCK
    nqb = tm // Q_BLOCK
    qspec = pl.BlockSpec((nqb, LANES, hq), lambda i: (i, 0, 0))
    return pl.pallas_call(
        _dsa_prep_kernel,
        grid=(t // tm,),
        in_specs=[pl.BlockSpec((tm, Q_RANK), lambda i: (i, F_ACQ // Q_RANK)),
                  pl.BlockSpec((tm, KV_RANK), lambda i: (i, F_ACKV // KV_RANK)),
                  pl.BlockSpec((tm, LANES), lambda i: (i, F_SMALL // LANES)),
                  _resident((SUBLANES, Q_RANK)),
                  _resident((SUBLANES, KV_RANK)),
                  _resident((SUBLANES, LANES)),
                  _resident((A_WIDTH, Q_RANK)),
                  _resident((IDX_HEADS, LANES, Q_RANK)),
                  _resident((A_HEADS, KV_RANK, A_HEAD_DIM))],
        out_specs=[qspec, qspec,
                   pl.BlockSpec((nqb, IDX_HEADS, Q_BLOCK), lambda i: (i, 0, 0)),
                   pl.BlockSpec((tm, LANES), lambda i: (i, 0)),
                   pl.BlockSpec((tm, KV_RANK), lambda i: (i, 0)),
                   pl.BlockSpec((tm // KEY_TILE, KV_RANK, KEY_TILE), lambda i: (i, 0, 0))],
        out_shape=[jax.ShapeDtypeStruct((t // Q_BLOCK, KV_RANK, hq), BF16),
                   jax.ShapeDtypeStruct((t // Q_BLOCK, LANES, hq), BF16),
                   jax.ShapeDtypeStruct((t // Q_BLOCK, IDX_HEADS, Q_BLOCK), F32),
                   jax.ShapeDtypeStruct((t, LANES), BF16),
                   jax.ShapeDtypeStruct((t, KV_RANK), BF16),
                   jax.ShapeDtypeStruct((t // KEY_TILE, KV_RANK, KEY_TILE), BF16)],
        compiler_params=_cparams(("parallel",)),
        name="dsa_prep",
    )(z, z, z, qn, kvn, kig, wuq_t, wuqi_t, wuk_t)


def _fold(x, op):
    parts = [x[i:i + SUBLANES] for i in range(0, x.shape[0], SUBLANES)]
    while len(parts) > 1:
        parts = [op(parts[i], parts[i + 1]) for i in range(0, len(parts), 2)]
    return parts[0]


def _tile_loop(n, body, init):
    def pair(j, carry):
        return body(2 * j + 1, body(2 * j, carry))
    carry = lax.fori_loop(0, n // 2, pair, init)
    return lax.fori_loop(2 * (n // 2), n, body, carry)


def _dsa_kernel(qidx_ref, qlat_ref, widx_ref, kidx_ref, ckv_ref, ckvt_ref, bias_ref, wuv_t_ref,
                out_ref, keys_scr, lg_scr, acc_scr, tie_scr, *, top_k, n_keys):
    QB = Q_BLOCK
    KT = KEY_TILE
    qblk = pl.program_id(1)
    nk = (qblk * QB + QB + KT - 1) // KT
    kf = float(top_k)
    hcols = [slice(h * QB, (h + 1) * QB) for h in range(A_HEADS)]
    hrows = [slice(h * SUBLANES, (h + 1) * SUBLANES) for h in range(A_HEADS)]

    q_pos = qblk * QB + lax.broadcasted_iota(I32, (KT, QB), 1)
    key_off = lax.broadcasted_iota(I32, (KT, QB), 0)

    qi_t = qidx_ref[0]
    w_rows = widx_ref[0]

    def score_body(kt, carry):
        s_t = _dot(kidx_ref[0, kt], qi_t)
        tot = jnp.zeros((KT, QB), F32)
        for g in range(IDX_HEADS):
            tot = tot + w_rows[g:g + 1, :] * jnp.maximum(s_t[:, hcols[g]], 0.0)
        key_pos = kt * KT + key_off
        tot = jnp.where(key_pos <= q_pos, tot + 0.0, -jnp.inf)
        bits = pltpu.bitcast(tot, I32)
        keys_scr[kt] = bits ^ ((bits >> 31) & 0x7FFFFFFF)
        return carry

    _tile_loop(nk, score_body, 0)

    def count(pred):
        def body(kt, acc):
            return acc + _fold(jnp.where(pred(keys_scr[kt], kt), 1.0, 0.0), jnp.add)
        acc = _tile_loop(nk, body, jnp.zeros((SUBLANES, QB), F32))
        return jnp.sum(acc, axis=0, keepdims=True)

    def count_ge(cand):
        return count(lambda keys, kt: keys >= cand)

    thr = jnp.where(count_ge(jnp.zeros((1, QB), I32)) >= kf, 0, INT_MIN).astype(I32)

    def bit_body(i, thr):
        cand = thr | (jnp.int32(1) << (30 - i))
        return jnp.where(count_ge(cand) >= kf, cand, thr)

    thr = lax.fori_loop(0, 31, bit_body, thr)

    tie_scr[...] = jnp.full((SUBLANES, QB), n_keys, I32)
    over = jnp.max(jnp.where(count_ge(thr) > kf, 1.0, 0.0))

    @pl.when(over > 0.0)
    def _():
        need = kf - count(lambda keys, kt: keys > thr)

        def idx_body(i, p):
            cand = p | (jnp.int32(1) << (n_keys.bit_length() - 1 - i))
            cnt = count(lambda keys, kt: (keys == thr) & (kt * KT + key_off < cand))
            return jnp.where(cnt < need, cand, p)

        p = lax.fori_loop(0, n_keys.bit_length(), idx_body, jnp.zeros((1, QB), I32))
        tie_scr[...] = jnp.broadcast_to(p, (SUBLANES, QB))

    tie_hi = tie_scr[0:1, :]

    ql_t = qlat_ref[0]

    def pass_a(kt, mx):
        lg_t = _dot(ckv_ref[0, kt], ql_t)
        keys = keys_scr[kt]
        key_pos = kt * KT + key_off
        sel = (keys > thr) | ((keys == thr) & (key_pos <= tie_hi))
        sel = sel & (key_pos <= q_pos)
        didx = jnp.minimum(qblk - (KT // QB) * kt, 3)
        new = []
        for h in range(A_HEADS):
            lh = jnp.where(sel, lg_t[:, hcols[h]] + bias_ref[didx, h], -jnp.inf)
            lg_scr[kt, :, hcols[h]] = lh
            new.append(jnp.maximum(mx[hrows[h]], _fold(lh, jnp.maximum)))
        return jnp.concatenate(new, axis=0)

    mx = _tile_loop(nk, pass_a, jnp.full((A_HEADS * SUBLANES, QB), -jnp.inf, F32))
    m_rows = [jnp.max(mx[hrows[h]], axis=0, keepdims=True) for h in range(A_HEADS)]

    acc_scr[...] = jnp.zeros_like(acc_scr)

    def pass_b(kt, l):
        lg = lg_scr[kt]
        ps, new = [], []
        for h in range(A_HEADS):
            p = jnp.exp(lg[:, hcols[h]] - m_rows[h])
            new.append(l[hrows[h]] + _fold(p, jnp.add))
            ps.append(p.astype(BF16))
        acc_scr[...] = acc_scr[...] + _dot(ckvt_ref[0, kt], jnp.concatenate(ps, axis=1))
        return jnp.concatenate(new, axis=0)

    l = _tile_loop(nk, pass_b, jnp.zeros((A_HEADS * SUBLANES, QB), F32))

    outs = []
    for h in range(A_HEADS):
        o_t = (acc_scr[:, hcols[h]] / jnp.sum(l[hrows[h]], axis=0, keepdims=True)).astype(BF16)
        outs.append(_dot(wuv_t_ref[h], o_t))
    out_ref[...] = jnp.concatenate(outs, axis=0).T.astype(BF16)


def _dsa(qidx_t, qlat_t, widx, kidx, ckv, ckv_t, bias_tab, wuv_t, b, s):
    nq = s // Q_BLOCK
    nt = s // KEY_TILE
    top_k = min(TOPK_MAX, s // 4)
    hq = A_HEADS * Q_BLOCK
    row = lambda bi, qi: bi * nq + qi
    qspec = pl.BlockSpec((1, LANES, hq), lambda bi, qi: (row(bi, qi), 0, 0))
    kspec = pl.BlockSpec((1, nt, KEY_TILE, LANES), lambda bi, qi: (bi, 0, 0, 0))
    return pl.pallas_call(
        functools.partial(_dsa_kernel, top_k=top_k, n_keys=s),
        grid=(b, nq),
        in_specs=[qspec, qspec,
                  pl.BlockSpec((1, IDX_HEADS, Q_BLOCK), lambda bi, qi: (row(bi, qi), 0, 0)),
                  kspec, kspec,
                  pl.BlockSpec((1, nt, KV_RANK, KEY_TILE), lambda bi, qi: (bi, 0, 0, 0)),
                  _resident((4, A_HEADS, KEY_TILE, Q_BLOCK)),
                  _resident((A_HEADS, A_HEAD_DIM, KV_RANK))],
        out_specs=pl.BlockSpec((Q_BLOCK, A_WIDTH), lambda bi, qi: (row(bi, qi), 0)),
        out_shape=jax.ShapeDtypeStruct((b * s, A_WIDTH), BF16),
        scratch_shapes=[pltpu.VMEM((nt, KEY_TILE, Q_BLOCK), I32),
                        pltpu.VMEM((nt, KEY_TILE, hq), F32),
                        pltpu.VMEM((KV_RANK, hq), F32),
                        pltpu.VMEM((SUBLANES, Q_BLOCK), I32)],
        compiler_params=_cparams(("parallel", "arbitrary")),
        name="dsa",
    )(qidx_t, qlat_t, widx, kidx.reshape(b, nt, KEY_TILE, LANES), ckv.reshape(b, nt, KEY_TILE, LANES),
      ckv_t.reshape(b, nt, KV_RANK, KEY_TILE), bias_tab, wuv_t)


FF_CHUNK = 256


def _merge_ffn_kernel(x_ref, hm_ref, ha_ref, hr_ref, gm_ref, ga_ref, gr_ref, pm_ref, pa_ref,
                      pr_ref, wo_ref, n2_ref, wg_ref, wu_ref, wd_ref, fg_ref, out_ref, *, final):
    y = (jax.nn.sigmoid(gm_ref[...].astype(F32)) * _dot(hm_ref[...], pm_ref[...])
         + jax.nn.sigmoid(ga_ref[...].astype(F32)) * _dot(ha_ref[...], pa_ref[...])
         + jax.nn.sigmoid(gr_ref[...].astype(F32)) * _dot(hr_ref[...], pr_ref[...]))
    x1 = x_ref[...] + _dot(y.astype(BF16), wo_ref[...])
    h2 = _rms(x1, n2_ref[0:1, :]).astype(BF16)
    acc = x1
    for c in range(D_FF // FF_CHUNK):
        cs = slice(c * FF_CHUNK, (c + 1) * FF_CHUNK)
        gate = _dot(h2, wg_ref[:, cs])
        up = _dot(h2, wu_ref[:, cs])
        act = (gate * jax.nn.sigmoid(gate) * up).astype(BF16)
        acc = acc + _dot(act, wd_ref[cs, :])
    if final:
        acc = _rms(acc, fg_ref[0:1, :])
    out_ref[...] = acc


def _merge_ffn(x2, hm, ha, hr, zb, pm, pa, pr, wo, n2, wg, wu, wd, fg, tm, final):
    t = x2.shape[0]
    rowspec = lambda w, cb=0: pl.BlockSpec((tm, w), lambda i: (i, cb))
    return pl.pallas_call(
        functools.partial(_merge_ffn_kernel, final=final),
        grid=(t // tm,),
        in_specs=[rowspec(D_MODEL), rowspec(M_WIDTH), rowspec(A_WIDTH), rowspec(R_WIDTH),
                  rowspec(D_MODEL, B_GM // D_MODEL), rowspec(D_MODEL, B_GA // D_MODEL),
                  rowspec(D_MODEL, B_GR // D_MODEL),
                  _resident((M_WIDTH, D_MODEL)), _resident((A_WIDTH, D_MODEL)),
                  _resident((R_WIDTH, D_MODEL)), _resident((D_MODEL, D_MODEL)),
                  _resident((SUBLANES, D_MODEL)),
                  _resident((D_MODEL, D_FF)), _resident((D_MODEL, D_FF)),
                  _resident((D_FF, D_MODEL)), _resident((SUBLANES, D_MODEL))],
        out_specs=rowspec(D_MODEL),
        out_shape=jax.ShapeDtypeStruct((t, D_MODEL), F32),
        compiler_params=_cparams(("parallel",)),
        name="merge_ffn",
    )(x2, hm, ha, hr, zb, zb, zb, pm, pa, pr, wo, n2, wg, wu, wd, fg)


def _pad_rows(v, rows=SUBLANES):
    v = jnp.atleast_2d(v).astype(F32)
    return jnp.pad(v, ((0, rows - v.shape[0]), (0, 0)))


def _t5_bucket(dist):
    max_exact = N_BUCKETS // 2
    d_f = jnp.maximum(dist, 1).astype(F32)
    large = max_exact + (jnp.log(d_f / max_exact) / math.log(MAX_DISTANCE / max_exact)
                         * (N_BUCKETS - max_exact)).astype(I32)
    large = jnp.minimum(large, N_BUCKETS - 1)
    return jnp.where(dist < max_exact, dist, large)


def _bias_table(rel_bias):
    d = jnp.arange(4, dtype=I32)[:, None, None]
    j = jnp.arange(KEY_TILE, dtype=I32)[None, :, None]
    i = jnp.arange(Q_BLOCK, dtype=I32)[None, None, :]
    bucket = _t5_bucket(jnp.maximum(i + d * Q_BLOCK - j, 0))
    onehot = (bucket[None] == jnp.arange(N_BUCKETS, dtype=I32)[:, None, None, None]).astype(F32)
    return jnp.einsum("nh,ndkq->dhkq", rel_bias.astype(F32), onehot, precision=lax.Precision.HIGHEST)


def _rotary_tables(s):
    half = R_QK // 2
    freqs = ROPE_BASE ** (-jnp.linspace(0.0, 1.0, half, dtype=F32))
    ang = jnp.arange(s, dtype=F32)[:, None] * freqs[None, :]
    cos, sin = jnp.cos(ang), jnp.sin(ang)
    return jnp.concatenate([cos, cos], axis=1), jnp.concatenate([-sin, sin], axis=1)


def _retention_tables():
    h = R_HEADS
    log_gamma = jnp.log1p(-jnp.exp2(-5.0 - jnp.arange(h, dtype=F32)))
    pos = jnp.arange(CHUNK, dtype=F32)
    diff = pos[:, None] - pos[None, :]
    intra = jnp.where(diff >= 0, jnp.exp(jnp.maximum(diff, 0.0)[None] * log_gamma[:, None, None]), 0.0)
    q_decay = jnp.exp((pos[None, :] + 1.0) * log_gamma[:, None])
    k_decay = jnp.exp((CHUNK - 1.0 - pos[None, :]) * log_gamma[:, None])
    chunk_decay = jnp.exp(CHUNK * log_gamma)
    qdec = jnp.broadcast_to(q_decay[:, :, None], (h, CHUNK, R_V))
    kdec = _pad_rows(k_decay)
    cdec = _pad_rows(jnp.broadcast_to(chunk_decay[:, None], (h, R_V)))
    return intra, qdec, kdec, cdec


def _layout_w_in(w_in):
    parts = jnp.split(w_in, [int(c) for c in np.cumsum(IN_SPLITS)[:-1]], axis=-1)
    (m_q, m_k, m_v, m_i, m_f, m_o, a_cq, a_kidx, a_widx, a_ckv,
     r_q, r_k, r_v, r_g, g_m, g_a, g_r) = parts
    small = jnp.concatenate([a_kidx, m_i, m_f, a_widx], axis=-1)
    small = jnp.pad(small, ((0, 0), (0, LANES - small.shape[-1])))
    w = jnp.concatenate([m_q, m_k, a_cq, a_ckv, small, r_q, r_k,
                         m_v, m_o, r_v, r_g, g_m, g_a, g_r], axis=-1)
    assert w.shape[-1] == ZF_WIDTH + ZB_WIDTH
    return w.astype(BF16)


def _row_tile(t, want):
    while t % want:
        want //= 2
    return want


def kernel(x, rel_bias, final_norm_g, norm1_g, w_in, m_conv, m_ibias, m_fbias, m_norm_g, a_qnorm_g, a_wuq, a_wuq_idx, a_kidx_g, a_kvnorm_g, a_wuk, a_wuv, r_norm_g, p_m, p_a, p_r, w_out, norm2_g, w_gate, w_up, w_down):
    b, s, d = x.shape
    depth = w_in.shape[0]
    assert d == D_MODEL and s % KEY_TILE == 0 and b % SCAN_ROWS == 0
    assert SM_F == SM_I + M_HEADS and M_V == LANES and R_V == LANES and CHUNK == LANES
    t = b * s
    tm_in = _row_tile(t, 1024)
    tm = _row_tile(t, 512)

    bias_tab = _bias_table(rel_bias)
    cosf, sinf = _rotary_tables(s)
    intra, qdec, kdec, cdec = _retention_tables()
    fg = _pad_rows(final_norm_g)

    x2 = x.reshape(t, d)
    for l in range(depth):
        zf, zb = _inproj(x2, norm1_g[l][None, :], _layout_w_in(w_in[l]), tm_in, 1024)

        gbias = jnp.zeros((LANES,), F32)
        gbias = gbias.at[SM_I:SM_I + M_HEADS].set(m_ibias[l]).at[SM_F:SM_F + M_HEADS].set(m_fbias[l])
        hm = _mlstm(zf, zb, _pad_rows(m_conv[l]), _pad_rows(gbias), _pad_rows(m_norm_g[l]), b, s)

        hr = _retention(zf, zb, cosf, sinf, intra, qdec, kdec, cdec, _pad_rows(r_norm_g[l]), b, s)

        wuqi_t = a_wuq_idx[l].reshape(Q_RANK, IDX_HEADS, IDX_DIM).transpose(1, 2, 0)
        wuqi_t = jnp.pad(wuqi_t, ((0, 0), (0, LANES - IDX_DIM), (0, 0))).astype(BF16)
        kig = jnp.pad(a_kidx_g[l], (0, LANES - IDX_DIM))
        qlat_t, qidx_t, widx, kidx, ckv, ckv_t = _dsa_prep(
            zf, _pad_rows(a_qnorm_g[l]), _pad_rows(a_kvnorm_g[l]), _pad_rows(kig),
            a_wuq[l].T.astype(BF16), wuqi_t, jnp.swapaxes(a_wuk[l], 1, 2).astype(BF16), tm)
        ha = _dsa(qidx_t, qlat_t, widx, kidx, ckv, ckv_t, bias_tab,
                  jnp.swapaxes(a_wuv[l], 1, 2).astype(BF16), b, s)

        x2 = _merge_ffn(x2, hm, ha, hr, zb, p_m[l].astype(BF16), p_a[l].astype(BF16),
                        p_r[l].astype(BF16), w_out[l].astype(BF16), _pad_rows(norm2_g[l]),
                        w_gate[l].astype(BF16), w_up[l].astype(BF16), w_down[l].astype(BF16),
                        fg, tm, final=(l == depth - 1))
    return x2.reshape(b, s, d)
```

```python
import functools
import math

import numpy as np
import jax
import jax.numpy as jnp
from jax import lax
from jax.experimental import pallas as pl
from jax.experimental.pallas import tpu as pltpu

F32 = jnp.float32
BF16 = jnp.bfloat16
I32 = jnp.int32
I16 = jnp.int16

D_MODEL = 1024
M_HEADS, M_QK, M_V = 4, 64, 128
M_WIDTH = M_HEADS * M_V
CONV_W = 4
A_HEADS, A_HEAD_DIM = 8, 64
A_WIDTH = A_HEADS * A_HEAD_DIM
Q_RANK, KV_RANK = 256, 128
IDX_HEADS, IDX_DIM = 8, 64
TOPK_MAX = 256
R_HEADS, R_QK, R_V = 4, 128, 128
R_WIDTH = R_HEADS * R_V
CHUNK = 128
Q_BLOCK = 128
N_BUCKETS = 32
MAX_DISTANCE = 128
ROPE_BASE = 10000.0
EPS = 1e-6
D_FF = -(((-8 * D_MODEL) // 3) // 256) * 256
IN_SPLITS = (M_HEADS * M_QK, M_HEADS * M_QK, M_WIDTH, M_HEADS, M_HEADS, M_WIDTH,
             Q_RANK, IDX_DIM, IDX_HEADS, KV_RANK,
             R_HEADS * R_QK, R_HEADS * R_QK, R_WIDTH, R_WIDTH,
             D_MODEL, D_MODEL, D_MODEL)

LANES = 128
SUBLANES = 8
VMEM_LIMIT = 56 * 1024 * 1024

F_MQK = 0
F_ACQ = 512
F_ACKV = 768
F_SMALL = 896
F_RQ, F_RK = 1024, 1536
ZF_WIDTH = 2048
B_MV, B_MO, B_RV, B_RG = 0, 512, 1024, 1536
B_GM, B_GA, B_GR = 2048, 3072, 4096
ZB_WIDTH = 5120
SM_KIDX, SM_I, SM_F, SM_W = 0, 64, 68, 72

KEY_TILE = 256
SCAN_ROWS = 2
HALF_MIN = -2 ** 15
HALF_ONE, HALF_ZERO = np.int16(1), np.int16(0)


def _cparams(sem):
    return pltpu.CompilerParams(dimension_semantics=sem, vmem_limit_bytes=VMEM_LIMIT)


def _resident(shape):
    nd = len(shape)
    return pl.BlockSpec(shape, lambda *_: (0,) * nd, pipeline_mode=pl.Buffered(1))


def _rms(x, g):
    return x * lax.rsqrt(jnp.mean(x * x, axis=-1, keepdims=True) + EPS) * g


def _group_norm(x, g):
    mu = jnp.mean(x, axis=-1, keepdims=True)
    xc = x - mu
    return xc * lax.rsqrt(jnp.mean(xc * xc, axis=-1, keepdims=True) + EPS) * g


def _dot(a, b):
    return jnp.dot(a, b, preferred_element_type=F32)


def _inproj_kernel(x_ref, g_ref, w_ref, zf_ref, zb_ref, h_scr, *, nf):
    j = pl.program_id(1)

    @pl.when(j == 0)
    def _():
        h_scr[...] = _rms(x_ref[...], g_ref[...]).astype(BF16)

    @pl.when(j < nf)
    def _():
        zf_ref[...] = _dot(h_scr[...], w_ref[...])

    @pl.when(j >= nf)
    def _():
        zb_ref[...] = _dot(h_scr[...], w_ref[...]).astype(BF16)


def _inproj(x2, g, w, tm, tn):
    t = x2.shape[0]
    nf = ZF_WIDTH // tn
    return pl.pallas_call(
        functools.partial(_inproj_kernel, nf=nf),
        grid=(t // tm, (ZF_WIDTH + ZB_WIDTH) // tn),
        in_specs=[pl.BlockSpec((tm, D_MODEL), lambda i, j: (i, 0)),
                  pl.BlockSpec((1, D_MODEL), lambda i, j: (0, 0)),
                  pl.BlockSpec((D_MODEL, tn), lambda i, j: (0, j))],
        out_specs=[pl.BlockSpec((tm, tn), lambda i, j: (i, jnp.minimum(j, nf - 1))),
                   pl.BlockSpec((tm, tn), lambda i, j: (i, jnp.maximum(j - nf, 0)))],
        out_shape=[jax.ShapeDtypeStruct((t, ZF_WIDTH), F32),
                   jax.ShapeDtypeStruct((t, ZB_WIDTH), BF16)],
        scratch_shapes=[pltpu.VMEM((tm, D_MODEL), BF16)],
        compiler_params=_cparams(("parallel", "arbitrary")),
        name="inproj",
    )(x2, g, w)


def _mlstm_kernel(qk_ref, v_ref, o_ref, sm_ref, conv_ref, gb_ref, ng_ref, out_ref,
                  xbuf, cst, mst):
    L = CHUNK
    c = pl.program_id(1)

    @pl.when(c == 0)
    def _():
        xbuf[:, 0:SUBLANES, :] = jnp.zeros((SCAN_ROWS, SUBLANES, 2 * M_HEADS * M_QK), F32)
        cst[...] = jnp.zeros_like(cst)
        mst[...] = jnp.zeros_like(mst)

    rows = range(SCAN_ROWS)
    pairs = [(r, h) for r in rows for h in range(M_HEADS)]
    n = len(pairs)
    r_i = lax.broadcasted_iota(I32, (L, L), 0)
    c_i = lax.broadcasted_iota(I32, (L, L), 1)
    causal = r_i >= c_i
    lower = causal.astype(F32)
    upper = (r_i <= c_i).astype(F32)
    lane = lax.broadcasted_iota(I32, (L, LANES), 1)
    is_f = (lane >= SM_F) & (lane < SM_F + M_HEADS)
    exact = dict(preferred_element_type=F32, precision=lax.Precision.HIGHEST)
    ones = jnp.ones((L, M_V), BF16)
    w = conv_ref[...]

    qk = []
    for r in rows:
        x = qk_ref[r, 0]
        xbuf[r, SUBLANES:SUBLANES + L, :] = x
        acc = w[CONV_W - 1:CONV_W, :] * x
        for j in range(CONV_W - 1):
            off = SUBLANES - (CONV_W - 1) + j
            acc = acc + w[j:j + 1, :] * xbuf[r, off:off + L, :]
        xbuf[r, 0:SUBLANES, :] = x[L - SUBLANES:L, :]
        qk.append(acc * jax.nn.sigmoid(acc))
    kt_all = [qk[r][:, M_HEADS * M_QK:].T for r in rows]

    g = [sm_ref[r, 0] + gb_ref[0:1, :] for r in rows]
    xg = [jnp.where(is_f, jax.nn.log_sigmoid(g[r]), g[r]) for r in rows]
    rows_if = [xg[r].T[SM_I:SM_I + 2 * M_HEADS, :] for r in rows]
    cum_r = [jnp.dot(rows_if[r], upper, **exact) for r in rows]
    bc = [jnp.dot(lower, xg[r], **exact) for r in rows]

    bc_q = [jnp.broadcast_to(bc[r][:, SM_F + h:SM_F + h + 1], (L, LANES)) for r, h in pairs]
    b_r = [rows_if[r][h:h + 1, :] - cum_r[r][M_HEADS + h:M_HEADS + h + 1, :] for r, h in pairs]
    m_prev = [mst[r, h:h + 1, :] for r, h in pairs]
    caug = [cst[r, h] for r, h in pairs]
    b_last = [bc_q[i][L - 1:L, :] for i in range(n)]
    d_log = [jnp.where(causal, bc_q[i] + b_r[i], -jnp.inf) for i in range(n)]
    inter = [bc_q[i] + m_prev[i] for i in range(n)]
    m_row = [jnp.maximum(inter[i], jnp.max(d_log[i], axis=-1, keepdims=True)) for i in range(n)]
    w_intra = [jnp.exp(d_log[i] - m_row[i]) for i in range(n)]
    w_inter = [jnp.exp(inter[i] - m_row[i]) for i in range(n)]
    floor = [jnp.exp(-m_row[i]) for i in range(n)]

    qb = [(qk[r][:, h * M_QK:(h + 1) * M_QK] * (M_QK ** -0.5)).astype(BF16) for r, h in pairs]
    kt = [kt_all[r][h * M_QK:(h + 1) * M_QK, :] for r, h in pairs]
    vaug = [jnp.concatenate([v_ref[r, 0, :, h * M_V:(h + 1) * M_V], ones], axis=1) for r, h in pairs]
    sc = [_dot(qb[i], kt[i].astype(BF16)) for i in range(n)]
    cross = [_dot(qb[i], caug[i].astype(BF16)) for i in range(n)]

    g_r = [b_last[i] + b_r[i] for i in range(n)]
    m_new = [jnp.maximum(b_last[i] + m_prev[i], jnp.max(g_r[i], axis=-1, keepdims=True)) for i in range(n)]
    wk_r = [jnp.exp(g_r[i] - m_new[i]) for i in range(n)]
    decay = [jnp.exp(b_last[i] + m_prev[i] - m_new[i]) for i in range(n)]
    upd = [_dot((kt[i] * wk_r[i]).astype(BF16), vaug[i]) for i in range(n)]

    sc = [(sc[i] * w_intra[i]).astype(BF16) for i in range(n)]
    num = [_dot(sc[i], vaug[i]) + jnp.concatenate([w_inter[i], w_inter[i]], axis=1) * cross[i]
           for i in range(n)]
    out = [num[i][:, :M_V] / jnp.maximum(jnp.abs(num[i][:, M_V:]), floor[i]) for i in range(n)]

    for i, (r, h) in enumerate(pairs):
        cst[r, h] = jnp.concatenate([decay[i], decay[i]], axis=1) * caug[i] + upd[i]
    for r in rows:
        mst[r, 0:M_HEADS, :] = jnp.concatenate(m_new[r * M_HEADS:(r + 1) * M_HEADS], axis=0)

    y = [_group_norm(out[i], ng_ref[0:1, h * M_V:(h + 1) * M_V]) for i, (r, h) in enumerate(pairs)]
    for i, (r, h) in enumerate(pairs):
        gate = jax.nn.sigmoid(o_ref[r, 0, :, h * M_V:(h + 1) * M_V].astype(F32))
        out_ref[r, 0, :, h * M_V:(h + 1) * M_V] = (y[i] * gate).astype(BF16)


def _mlstm(zf, zb, conv_w, gbias, ng, b, s):
    nc = s // CHUNK
    w512 = 2 * M_HEADS * M_QK
    zf4 = zf.reshape(b, nc, CHUNK, ZF_WIDTH)
    zb4 = zb.reshape(b, nc, CHUNK, ZB_WIDTH)
    zspec = lambda w, off: pl.BlockSpec((SCAN_ROWS, 1, CHUNK, w), lambda bi, ci: (bi, ci, 0, off // w))
    out = pl.pallas_call(
        _mlstm_kernel,
        grid=(b // SCAN_ROWS, nc),
        in_specs=[zspec(w512, F_MQK), zspec(M_WIDTH, B_MV), zspec(M_WIDTH, B_MO), zspec(LANES, F_SMALL),
                  _resident((SUBLANES, w512)),
                  _resident((SUBLANES, LANES)),
                  _resident((SUBLANES, M_WIDTH))],
        out_specs=pl.BlockSpec((SCAN_ROWS, 1, CHUNK, M_WIDTH), lambda bi, ci: (bi, ci, 0, 0)),
        out_shape=jax.ShapeDtypeStruct((b, nc, CHUNK, M_WIDTH), BF16),
        scratch_shapes=[pltpu.VMEM((SCAN_ROWS, SUBLANES + CHUNK, w512), F32),
                        pltpu.VMEM((SCAN_ROWS, M_HEADS, M_QK, 2 * M_V), F32),
                        pltpu.VMEM((SCAN_ROWS, SUBLANES, LANES), F32)],
        compiler_params=_cparams(("parallel", "arbitrary")),
        name="mlstm",
    )(zf4, zb4, zb4, zf4, conv_w, gbias, ng)
    return out.reshape(b * s, M_WIDTH)


def _ret_kernel(q_ref, k_ref, v_ref, g_ref, cos_ref, sin_ref, intra_ref, qdec_ref, kdec_ref,
                cdec_ref, ng_ref, out_ref, st):
    c = pl.program_id(1)

    @pl.when(c == 0)
    def _():
        st[...] = jnp.zeros_like(st)

    pairs = [(r, h) for r in range(SCAN_ROWS) for h in range(R_HEADS)]
    sl = [slice(h * R_QK, (h + 1) * R_QK) for h in range(R_HEADS)]
    cosf = cos_ref[...]
    sinf = sin_ref[...]
    half = R_QK // 2

    def rot(x):
        return x * cosf + pltpu.roll(x, half, 1) * sinf

    qr = [rot(q_ref[r, 0, :, sl[h]]).astype(BF16) for r, h in pairs]
    kt = [(rot(k_ref[r, 0, :, sl[h]]) * (R_QK ** -0.5)).T for r, h in pairs]
    vb = [v_ref[r, 0, :, sl[h]] for r, h in pairs]
    state = [st[r, h] for r, h in pairs]
    sc = [_dot(qr[i], kt[i].astype(BF16)) for i in range(len(pairs))]
    cross = [_dot(qr[i], state[i].astype(BF16)) for i in range(len(pairs))]
    upd = [_dot((kt[i] * kdec_ref[h:h + 1, :]).astype(BF16), vb[i]) for i, (r, h) in enumerate(pairs)]
    sc = [(sc[i] * intra_ref[h]).astype(BF16) for i, (r, h) in enumerate(pairs)]
    out = [_dot(sc[i], vb[i]) + qdec_ref[h] * cross[i] for i, (r, h) in enumerate(pairs)]
    for i, (r, h) in enumerate(pairs):
        st[r, h] = cdec_ref[h:h + 1, :] * state[i] + upd[i]
    y = [_group_norm(out[i], ng_ref[0:1, sl[h]]) for i, (r, h) in enumerate(pairs)]
    for i, (r, h) in enumerate(pairs):
        gt = g_ref[r, 0, :, sl[h]].astype(F32)
        out_ref[r, 0, :, sl[h]] = (y[i] * (gt * jax.nn.sigmoid(gt))).astype(BF16)


def _retention(zf, zb, cosf, sinf, intra, qdec, kdec, cdec, ng, b, s):
    nc = s // CHUNK
    zf4 = zf.reshape(b, nc, CHUNK, ZF_WIDTH)
    zb4 = zb.reshape(b, nc, CHUNK, ZB_WIDTH)
    zspec = lambda off: pl.BlockSpec((SCAN_ROWS, 1, CHUNK, R_WIDTH),
                                     lambda bi, ci: (bi, ci, 0, off // R_WIDTH))
    out = pl.pallas_call(
        _ret_kernel,
        grid=(b // SCAN_ROWS, nc),
        in_specs=[zspec(F_RQ), zspec(F_RK), zspec(B_RV), zspec(B_RG),
                  pl.BlockSpec((CHUNK, R_QK), lambda bi, ci: (ci, 0)),
                  pl.BlockSpec((CHUNK, R_QK), lambda bi, ci: (ci, 0)),
                  _resident((R_HEADS, CHUNK, CHUNK)),
                  _resident((R_HEADS, CHUNK, R_V)),
                  _resident((SUBLANES, CHUNK)),
                  _resident((SUBLANES, R_V)),
                  _resident((SUBLANES, R_WIDTH))],
        out_specs=pl.BlockSpec((SCAN_ROWS, 1, CHUNK, R_WIDTH), lambda bi, ci: (bi, ci, 0, 0)),
        out_shape=jax.ShapeDtypeStruct((b, nc, CHUNK, R_WIDTH), BF16),
        scratch_shapes=[pltpu.VMEM((SCAN_ROWS, R_HEADS, R_QK, R_V), F32)],
        compiler_params=_cparams(("parallel", "arbitrary")),
        name="retention",
    )(zf4, zf4, zb4, zb4, cosf, sinf, intra, qdec, kdec, cdec, ng)
    return out.reshape(b * s, R_WIDTH)


def _dsa_prep_kernel(cq_ref, ckv_ref, sm_ref, qn_ref, kvn_ref, kig_ref, wuq_t_ref, wuqi_t_ref,
                     wuk_t_ref, qlat_ref, qidx_ref, widx_ref, kidx_ref, ckvn_ref, ckvt_ref):
    tm = cq_ref.shape[0]
    c_qt = _rms(cq_ref[...], qn_ref[0:1, :]).T.astype(BF16)
    q_t = _dot(wuq_t_ref[...], c_qt).astype(BF16)
    for h in range(A_HEADS):
        ql_t = (_dot(wuk_t_ref[h], q_t[h * A_HEAD_DIM:(h + 1) * A_HEAD_DIM, :])
                * (A_HEAD_DIM ** -0.5)).astype(BF16)
        qi_t = _dot(wuqi_t_ref[h], c_qt).astype(BF16)
        for j in range(tm // Q_BLOCK):
            cols = slice(j * Q_BLOCK, (j + 1) * Q_BLOCK)
            qlat_ref[j, :, h * Q_BLOCK:(h + 1) * Q_BLOCK] = ql_t[:, cols]
            qidx_ref[j, :, h * Q_BLOCK:(h + 1) * Q_BLOCK] = qi_t[:, cols]
    sm = sm_ref[...]
    w_t = sm.T[SM_W:SM_W + IDX_HEADS, :] * (IDX_HEADS ** -0.5 * IDX_DIM ** -0.5)
    for j in range(tm // Q_BLOCK):
        widx_ref[j] = w_t[:, j * Q_BLOCK:(j + 1) * Q_BLOCK]
    lane = lax.broadcasted_iota(I32, sm.shape, 1)
    inside = lane < IDX_DIM
    mu = jnp.sum(jnp.where(inside, sm, 0.0), axis=-1, keepdims=True) * (1.0 / IDX_DIM)
    xc = jnp.where(inside, sm - mu, 0.0)
    var = jnp.sum(xc * xc, axis=-1, keepdims=True) * (1.0 / IDX_DIM)
    kidx_ref[...] = (xc * lax.rsqrt(var + EPS) * kig_ref[0:1, :]).astype(BF16)
    ckvn = _rms(ckv_ref[...], kvn_ref[0:1, :])
    ckvn_ref[...] = ckvn.astype(BF16)
    ckvn_t = ckvn.T.astype(BF16)
    for j in range(tm // KEY_TILE):
        ckvt_ref[j] = ckvn_t[:, j * KEY_TILE:(j + 1) * KEY_TILE]


def _dsa_prep(z, qn, kvn, kig, wuq_t, wuqi_t, wuk_t, tm):
    t = z.shape[0]
    hq = A_HEADS * Q_BLOCK
    nqb = tm // Q_BLOCK
    qspec = pl.BlockSpec((nqb, LANES, hq), lambda i: (i, 0, 0))
    return pl.pallas_call(
        _dsa_prep_kernel,
        grid=(t // tm,),
        in_specs=[pl.BlockSpec((tm, Q_RANK), lambda i: (i, F_ACQ // Q_RANK)),
                  pl.BlockSpec((tm, KV_RANK), lambda i: (i, F_ACKV // KV_RANK)),
                  pl.BlockSpec((tm, LANES), lambda i: (i, F_SMALL // LANES)),
                  _resident((SUBLANES, Q_RANK)),
                  _resident((SUBLANES, KV_RANK)),
                  _resident((SUBLANES, LANES)),
                  _resident((A_WIDTH, Q_RANK)),
                  _resident((IDX_HEADS, LANES, Q_RANK)),
                  _resident((A_HEADS, KV_RANK, A_HEAD_DIM))],
        out_specs=[qspec, qspec,
                   pl.BlockSpec((nqb, IDX_HEADS, Q_BLOCK), lambda i: (i, 0, 0)),
                   pl.BlockSpec((tm, LANES), lambda i: (i, 0)),
                   pl.BlockSpec((tm, KV_RANK), lambda i: (i, 0)),
                   pl.BlockSpec((tm // KEY_TILE, KV_RANK, KEY_TILE), lambda i: (i, 0, 0))],
        out_shape=[jax.ShapeDtypeStruct((t // Q_BLOCK, KV_RANK, hq), BF16),
                   jax.ShapeDtypeStruct((t // Q_BLOCK, LANES, hq), BF16),
                   jax.ShapeDtypeStruct((t // Q_BLOCK, IDX_HEADS, Q_BLOCK), F32),
                   jax.ShapeDtypeStruct((t, LANES), BF16),
                   jax.ShapeDtypeStruct((t, KV_RANK), BF16),
                   jax.ShapeDtypeStruct((t // KEY_TILE, KV_RANK, KEY_TILE), BF16)],
        compiler_params=_cparams(("parallel",)),
        name="dsa_prep",
    )(z, z, z, qn, kvn, kig, wuq_t, wuqi_t, wuk_t)


def _fold(x, op, rows=SUBLANES):
    parts = [x[i:i + rows] for i in range(0, x.shape[0], rows)]
    while len(parts) > 1:
        parts = [op(parts[i], parts[i + 1]) for i in range(0, len(parts), 2)]
    return parts[0]


def _tile_loop(n, body, init, quads=False):
    def pair(j, carry):
        return body(j + 1, body(j, carry))

    def quad(j, carry):
        return pair(4 * j + 2, pair(4 * j, carry))

    start = 0
    carry = init
    if quads:
        carry = lax.fori_loop(0, n // 4, quad, carry)
        start = 4 * (n // 4)
    npair = (n - start) // 2
    carry = lax.fori_loop(0, npair, lambda j, c: pair(start + 2 * j, c), carry)
    return lax.fori_loop(start + 2 * npair, n, body, carry)


def _dsa_kernel(qidx_ref, qlat_ref, widx_ref, kidx_ref, ckv_ref, ckvt_ref, bias_ref, wuv_t_ref,
                out_ref, keys_scr, hi_scr, lo_scr, lg_scr, acc_scr, tie_scr, *, top_k, n_keys):
    QB = Q_BLOCK
    KT = KEY_TILE
    qblk = pl.program_id(1)
    nk = (qblk * QB + QB + KT - 1) // KT
    kf = float(top_k)
    hcols = [slice(h * QB, (h + 1) * QB) for h in range(A_HEADS)]
    hrows = [slice(h * SUBLANES, (h + 1) * SUBLANES) for h in range(A_HEADS)]

    q_pos = qblk * QB + lax.broadcasted_iota(I32, (KT, QB), 1)
    key_off = lax.broadcasted_iota(I32, (KT, QB), 0)

    qi_t = qidx_ref[0]
    w_rows = widx_ref[0]

    def score_body(kt, carry):
        s_t = _dot(kidx_ref[0, kt], qi_t)
        tot = jnp.zeros((KT, QB), F32)
        for g in range(IDX_HEADS):
            tot = tot + w_rows[g:g + 1, :] * jnp.maximum(s_t[:, hcols[g]], 0.0)
        key_pos = kt * KT + key_off
        tot = jnp.where(key_pos <= q_pos, tot + 0.0, -jnp.inf)
        bits = pltpu.bitcast(tot, I32)
        keys = bits ^ ((bits >> 31) & 0x7FFFFFFF)
        keys_scr[kt] = keys
        hi_scr[kt] = (keys >> 16).astype(I16)
        lo_scr[kt] = ((keys & 0xFFFF) + HALF_MIN).astype(I16)
        return carry

    _tile_loop(nk, score_body, 0, quads=True)

    def count16(scr, pred):
        def body(kt, acc):
            return acc + _fold(jnp.where(pred(scr[kt]), HALF_ONE, HALF_ZERO), jnp.add, 2 * SUBLANES)
        acc = _tile_loop(nk, body, jnp.zeros((2 * SUBLANES, QB), I16))
        return jnp.sum(acc.astype(F32), axis=0, keepdims=True)

    def kth_half(scr, need):
        def ge(cand):
            c16 = cand.astype(I16)
            return count16(scr, lambda x: x >= c16)
        v = jnp.where(ge(jnp.zeros((1, QB), I32)) >= need, 0, HALF_MIN).astype(I32)

        def bit_body(i, v):
            cand = v | (jnp.int32(1) << (14 - i))
            return jnp.where(ge(cand) >= need, cand, v)

        return lax.fori_loop(0, 15, bit_body, v)

    hi_t = kth_half(hi_scr, kf)
    hi16 = hi_t.astype(I16)
    need_lo = kf - count16(hi_scr, lambda x: x > hi16)

    def mask_lo(kt, carry):
        lo_scr[kt] = jnp.where(hi_scr[kt] == hi16, lo_scr[kt], np.int16(HALF_MIN))
        return carry

    _tile_loop(nk, mask_lo, 0)
    lo_t = kth_half(lo_scr, need_lo)
    thr = (hi_t << 16) | (lo_t - HALF_MIN)

    def count(pred):
        def body(kt, acc):
            return acc + _fold(jnp.where(pred(keys_scr[kt], kt), 1.0, 0.0), jnp.add)
        acc = _tile_loop(nk, body, jnp.zeros((SUBLANES, QB), F32))
        return jnp.sum(acc, axis=0, keepdims=True)

    tie_scr[...] = jnp.full((SUBLANES, QB), n_keys, I32)
    lo16 = lo_t.astype(I16)
    over = jnp.max(jnp.where(count16(lo_scr, lambda x: x >= lo16) > need_lo, 1.0, 0.0))

    @pl.when(over > 0.0)
    def _():
        need = kf - count(lambda keys, kt: keys > thr)

        def idx_body(i, p):
            cand = p | (jnp.int32(1) << (n_keys.bit_length() - 1 - i))
            cnt = count(lambda keys, kt: (keys == thr) & (kt * KT + key_off < cand))
            return jnp.where(cnt < need, cand, p)

        p = lax.fori_loop(0, n_keys.bit_length(), idx_body, jnp.zeros((1, QB), I32))
        tie_scr[...] = jnp.broadcast_to(p, (SUBLANES, QB))

    tie_hi = tie_scr[0:1, :]

    ql_t = qlat_ref[0]

    def pass_a(kt, mx):
        lg_t = _dot(ckv_ref[0, kt], ql_t)
        keys = keys_scr[kt]
        key_pos = kt * KT + key_off
        sel = (keys > thr) | ((keys == thr) & (key_pos <= tie_hi))
        sel = sel & (key_pos <= q_pos)
        didx = jnp.minimum(qblk - (KT // QB) * kt, 3)
        new = []
        for h in range(A_HEADS):
            lh = jnp.where(sel, lg_t[:, hcols[h]] + bias_ref[didx, h], -jnp.inf)
            lg_scr[kt, :, hcols[h]] = lh
            new.append(jnp.maximum(mx[hrows[h]], _fold(lh, jnp.maximum)))
        return jnp.concatenate(new, axis=0)

    mx = _tile_loop(nk, pass_a, jnp.full((A_HEADS * SUBLANES, QB), -jnp.inf, F32), quads=True)
    m_rows = [jnp.max(mx[hrows[h]], axis=0, keepdims=True) for h in range(A_HEADS)]

    acc_scr[...] = jnp.zeros_like(acc_scr)

    def pass_b(kt, l):
        lg = lg_scr[kt]
        ps, new = [], []
        for h in range(A_HEADS):
            p = jnp.exp(lg[:, hcols[h]] - m_rows[h])
            new.append(l[hrows[h]] + _fold(p, jnp.add))
            ps.append(p.astype(BF16))
        acc_scr[...] = acc_scr[...] + _dot(ckvt_ref[0, kt], jnp.concatenate(ps, axis=1))
        return jnp.concatenate(new, axis=0)

    l = _tile_loop(nk, pass_b, jnp.zeros((A_HEADS * SUBLANES, QB), F32), quads=True)

    outs = []
    for h in range(A_HEADS):
        o_t = (acc_scr[:, hcols[h]] / jnp.sum(l[hrows[h]], axis=0, keepdims=True)).astype(BF16)
        outs.append(_dot(wuv_t_ref[h], o_t))
    out_ref[...] = jnp.concatenate(outs, axis=0).T.astype(BF16)


def _dsa(qidx_t, qlat_t, widx, kidx, ckv, ckv_t, bias_tab, wuv_t, b, s):
    nq = s // Q_BLOCK
    nt = s // KEY_TILE
    top_k = min(TOPK_MAX, s // 4)
    hq = A_HEADS * Q_BLOCK
    row = lambda bi, qi: bi * nq + qi
    qspec = pl.BlockSpec((1, LANES, hq), lambda bi, qi: (row(bi, qi), 0, 0))
    kspec = pl.BlockSpec((1, nt, KEY_TILE, LANES), lambda bi, qi: (bi, 0, 0, 0))
    return pl.pallas_call(
        functools.partial(_dsa_kernel, top_k=top_k, n_keys=s),
        grid=(b, nq),
        in_specs=[qspec, qspec,
                  pl.BlockSpec((1, IDX_HEADS, Q_BLOCK), lambda bi, qi: (row(bi, qi), 0, 0)),
                  kspec, kspec,
                  pl.BlockSpec((1, nt, KV_RANK, KEY_TILE), lambda bi, qi: (bi, 0, 0, 0)),
                  _resident((4, A_HEADS, KEY_TILE, Q_BLOCK)),
                  _resident((A_HEADS, A_HEAD_DIM, KV_RANK))],
        out_specs=pl.BlockSpec((Q_BLOCK, A_WIDTH), lambda bi, qi: (row(bi, qi), 0)),
        out_shape=jax.ShapeDtypeStruct((b * s, A_WIDTH), BF16),
        scratch_shapes=[pltpu.VMEM((nt, KEY_TILE, Q_BLOCK), I32),
                        pltpu.VMEM((nt, KEY_TILE, Q_BLOCK), I16),
                        pltpu.VMEM((nt, KEY_TILE, Q_BLOCK), I16),
                        pltpu.VMEM((nt, KEY_TILE, hq), F32),
                        pltpu.VMEM((KV_RANK, hq), F32),
                        pltpu.VMEM((SUBLANES, Q_BLOCK), I32)],
        compiler_params=_cparams(("parallel", "arbitrary")),
        name="dsa",
    )(qidx_t, qlat_t, widx, kidx.reshape(b, nt, KEY_TILE, LANES), ckv.reshape(b, nt, KEY_TILE, LANES),
      ckv_t.reshape(b, nt, KV_RANK, KEY_TILE), bias_tab, wuv_t)


FF_CHUNK = 256


def _merge_ffn_kernel(x_ref, hm_ref, ha_ref, hr_ref, gm_ref, ga_ref, gr_ref, pm_ref, pa_ref,
                      pr_ref, wo_ref, n2_ref, wg_ref, wu_ref, wd_ref, fg_ref, out_ref, *, final):
    y = (jax.nn.sigmoid(gm_ref[...].astype(F32)) * _dot(hm_ref[...], pm_ref[...])
         + jax.nn.sigmoid(ga_ref[...].astype(F32)) * _dot(ha_ref[...], pa_ref[...])
         + jax.nn.sigmoid(gr_ref[...].astype(F32)) * _dot(hr_ref[...], pr_ref[...]))
    x1 = x_ref[...] + _dot(y.astype(BF16), wo_ref[...])
    h2 = _rms(x1, n2_ref[0:1, :]).astype(BF16)
    acc = x1
    for c in range(D_FF // FF_CHUNK):
        cs = slice(c * FF_CHUNK, (c + 1) * FF_CHUNK)
        gate = _dot(h2, wg_ref[:, cs])
        up = _dot(h2, wu_ref[:, cs])
        act = (gate * jax.nn.sigmoid(gate) * up).astype(BF16)
        acc = acc + _dot(act, wd_ref[cs, :])
    if final:
        acc = _rms(acc, fg_ref[0:1, :])
    out_ref[...] = acc


def _merge_ffn(x2, hm, ha, hr, zb, pm, pa, pr, wo, n2, wg, wu, wd, fg, tm, final):
    t = x2.shape[0]
    rowspec = lambda w, cb=0: pl.BlockSpec((tm, w), lambda i: (i, cb))
    return pl.pallas_call(
        functools.partial(_merge_ffn_kernel, final=final),
        grid=(t // tm,),
        in_specs=[rowspec(D_MODEL), rowspec(M_WIDTH), rowspec(A_WIDTH), rowspec(R_WIDTH),
                  rowspec(D_MODEL, B_GM // D_MODEL), rowspec(D_MODEL, B_GA // D_MODEL),
                  rowspec(D_MODEL, B_GR // D_MODEL),
                  _resident((M_WIDTH, D_MODEL)), _resident((A_WIDTH, D_MODEL)),
                  _resident((R_WIDTH, D_MODEL)), _resident((D_MODEL, D_MODEL)),
                  _resident((SUBLANES, D_MODEL)),
                  _resident((D_MODEL, D_FF)), _resident((D_MODEL, D_FF)),
                  _resident((D_FF, D_MODEL)), _resident((SUBLANES, D_MODEL))],
        out_specs=rowspec(D_MODEL),
        out_shape=jax.ShapeDtypeStruct((t, D_MODEL), F32),
        compiler_params=_cparams(("parallel",)),
        name="merge_ffn",
    )(x2, hm, ha, hr, zb, zb, zb, pm, pa, pr, wo, n2, wg, wu, wd, fg)


def _pad_rows(v, rows=SUBLANES):
    v = jnp.atleast_2d(v).astype(F32)
    return jnp.pad(v, ((0, rows - v.shape[0]), (0, 0)))


def _t5_bucket(dist):
    max_exact = N_BUCKETS // 2
    d_f = jnp.maximum(dist, 1).astype(F32)
    large = max_exact + (jnp.log(d_f / max_exact) / math.log(MAX_DISTANCE / max_exact)
                         * (N_BUCKETS - max_exact)).astype(I32)
    large = jnp.minimum(large, N_BUCKETS - 1)
    return jnp.where(dist < max_exact, dist, large)


def _bias_table(rel_bias):
    d = jnp.arange(4, dtype=I32)[:, None, None]
    j = jnp.arange(KEY_TILE, dtype=I32)[None, :, None]
    i = jnp.arange(Q_BLOCK, dtype=I32)[None, None, :]
    bucket = _t5_bucket(jnp.maximum(i + d * Q_BLOCK - j, 0))
    onehot = (bucket[None] == jnp.arange(N_BUCKETS, dtype=I32)[:, None, None, None]).astype(F32)
    return jnp.einsum("nh,ndkq->dhkq", rel_bias.astype(F32), onehot, precision=lax.Precision.HIGHEST)


def _rotary_tables(s):
    half = R_QK // 2
    freqs = ROPE_BASE ** (-jnp.linspace(0.0, 1.0, half, dtype=F32))
    ang = jnp.arange(s, dtype=F32)[:, None] * freqs[None, :]
    cos, sin = jnp.cos(ang), jnp.sin(ang)
    return jnp.concatenate([cos, cos], axis=1), jnp.concatenate([-sin, sin], axis=1)


def _retention_tables():
    h = R_HEADS
    log_gamma = jnp.log1p(-jnp.exp2(-5.0 - jnp.arange(h, dtype=F32)))
    pos = jnp.arange(CHUNK, dtype=F32)
    diff = pos[:, None] - pos[None, :]
    intra = jnp.where(diff >= 0, jnp.exp(jnp.maximum(diff, 0.0)[None] * log_gamma[:, None, None]), 0.0)
    q_decay = jnp.exp((pos[None, :] + 1.0) * log_gamma[:, None])
    k_decay = jnp.exp((CHUNK - 1.0 - pos[None, :]) * log_gamma[:, None])
    chunk_decay = jnp.exp(CHUNK * log_gamma)
    qdec = jnp.broadcast_to(q_decay[:, :, None], (h, CHUNK, R_V))
    kdec = _pad_rows(k_decay)
    cdec = _pad_rows(jnp.broadcast_to(chunk_decay[:, None], (h, R_V)))
    return intra, qdec, kdec, cdec


def _layout_w_in(w_in):
    parts = jnp.split(w_in, [int(c) for c in np.cumsum(IN_SPLITS)[:-1]], axis=-1)
    (m_q, m_k, m_v, m_i, m_f, m_o, a_cq, a_kidx, a_widx, a_ckv,
     r_q, r_k, r_v, r_g, g_m, g_a, g_r) = parts
    small = jnp.concatenate([a_kidx, m_i, m_f, a_widx], axis=-1)
    small = jnp.pad(small, ((0, 0), (0, LANES - small.shape[-1])))
    w = jnp.concatenate([m_q, m_k, a_cq, a_ckv, small, r_q, r_k,
                         m_v, m_o, r_v, r_g, g_m, g_a, g_r], axis=-1)
    assert w.shape[-1] == ZF_WIDTH + ZB_WIDTH
    return w.astype(BF16)


def _row_tile(t, want):
    while t % want:
        want //= 2
    return want


def kernel(x, rel_bias, final_norm_g, norm1_g, w_in, m_conv, m_ibias, m_fbias, m_norm_g, a_qnorm_g, a_wuq, a_wuq_idx, a_kidx_g, a_kvnorm_g, a_wuk, a_wuv, r_norm_g, p_m, p_a, p_r, w_out, norm2_g, w_gate, w_up, w_down):
    b, s, d = x.shape
    depth = w_in.shape[0]
    assert d == D_MODEL and s % KEY_TILE == 0 and b % SCAN_ROWS == 0
    assert SM_F == SM_I + M_HEADS and M_V == LANES and R_V == LANES and CHUNK == LANES
    t = b * s
    tm_in = _row_tile(t, 2048)
    tm = _row_tile(t, 512)

    bias_tab = _bias_table(rel_bias)
    cosf, sinf = _rotary_tables(s)
    intra, qdec, kdec, cdec = _retention_tables()
    fg = _pad_rows(final_norm_g)

    x2 = x.reshape(t, d)
    for l in range(depth):
        zf, zb = _inproj(x2, norm1_g[l][None, :], _layout_w_in(w_in[l]), tm_in, 1024)

        gbias = jnp.zeros((LANES,), F32)
        gbias = gbias.at[SM_I:SM_I + M_HEADS].set(m_ibias[l]).at[SM_F:SM_F + M_HEADS].set(m_fbias[l])
        hm = _mlstm(zf, zb, _pad_rows(m_conv[l]), _pad_rows(gbias), _pad_rows(m_norm_g[l]), b, s)

        hr = _retention(zf, zb, cosf, sinf, intra, qdec, kdec, cdec, _pad_rows(r_norm_g[l]), b, s)

        wuqi_t = a_wuq_idx[l].reshape(Q_RANK, IDX_HEADS, IDX_DIM).transpose(1, 2, 0)
        wuqi_t = jnp.pad(wuqi_t, ((0, 0), (0, LANES - IDX_DIM), (0, 0))).astype(BF16)
        kig = jnp.pad(a_kidx_g[l], (0, LANES - IDX_DIM))
        qlat_t, qidx_t, widx, kidx, ckv, ckv_t = _dsa_prep(
            zf, _pad_rows(a_qnorm_g[l]), _pad_rows(a_kvnorm_g[l]), _pad_rows(kig),
            a_wuq[l].T.astype(BF16), wuqi_t, jnp.swapaxes(a_wuk[l], 1, 2).astype(BF16), tm)
        ha = _dsa(qidx_t, qlat_t, widx, kidx, ckv, ckv_t, bias_tab,
                  jnp.swapaxes(a_wuv[l], 1, 2).astype(BF16), b, s)

        x2 = _merge_ffn(x2, hm, ha, hr, zb, p_m[l].astype(BF16), p_a[l].astype(BF16),
                        p_r[l].astype(BF16), w_out[l].astype(BF16), _pad_rows(norm2_g[l]),
                        w_gate[l].astype(BF16), w_up[l].astype(BF16), w_down[l].astype(BF16),
                        fg, tm, final=(l == depth - 1))
    return x2.reshape(b, s, d)
```

```python
import functools
import math

import numpy as np
import jax
import jax.numpy as jnp
from jax import lax
from jax.experimental import pallas as pl
from jax.experimental.pallas import tpu as pltpu

F32 = jnp.float32
BF16 = jnp.bfloat16
I32 = jnp.int32

D_MODEL = 1024
M_HEADS, M_QK, M_V = 4, 64, 128
M_WIDTH = M_HEADS * M_V
CONV_W = 4
A_HEADS, A_HEAD_DIM = 8, 64
A_WIDTH = A_HEADS * A_HEAD_DIM
Q_RANK, KV_RANK = 256, 128
IDX_HEADS, IDX_DIM = 8, 64
TOPK_MAX = 256
R_HEADS, R_QK, R_V = 4, 128, 128
R_WIDTH = R_HEADS * R_V
CHUNK = 128
Q_BLOCK = 128
N_BUCKETS = 32
MAX_DISTANCE = 128
ROPE_BASE = 10000.0
EPS = 1e-6
D_FF = -(((-8 * D_MODEL) // 3) // 256) * 256
IN_SPLITS = (M_HEADS * M_QK, M_HEADS * M_QK, M_WIDTH, M_HEADS, M_HEADS, M_WIDTH,
             Q_RANK, IDX_DIM, IDX_HEADS, KV_RANK,
             R_HEADS * R_QK, R_HEADS * R_QK, R_WIDTH, R_WIDTH,
             D_MODEL, D_MODEL, D_MODEL)

LANES = 128
SUBLANES = 8
VMEM_LIMIT = 56 * 1024 * 1024
VMEM_LIMIT_TAIL = 62 * 1024 * 1024

F_MQK = 0
F_ACQ = 512
F_ACKV = 768
F_SMALL = 896
F_RQ, F_RK = 1024, 1536
ZF_WIDTH = 2048
B_MV, B_MO, B_RV, B_RG = 0, 512, 1024, 1536
B_GM, B_GA, B_GR = 2048, 3072, 4096
ZB_WIDTH = 5120
SM_KIDX, SM_I, SM_F, SM_W = 0, 64, 68, 72

KEY_TILE = 256
INT_MIN = -2 ** 31


def _cparams(sem, vmem_limit=VMEM_LIMIT):
    return pltpu.CompilerParams(dimension_semantics=sem, vmem_limit_bytes=vmem_limit)


def _resident(shape):
    nd = len(shape)
    return pl.BlockSpec(shape, lambda *_: (0,) * nd, pipeline_mode=pl.Buffered(1))


def _rms(x, g):
    return x * lax.rsqrt(jnp.mean(x * x, axis=-1, keepdims=True) + EPS) * g


def _group_norm(x, g):
    mu = jnp.mean(x, axis=-1, keepdims=True)
    xc = x - mu
    return xc * lax.rsqrt(jnp.mean(xc * xc, axis=-1, keepdims=True) + EPS) * g


def _dot(a, b):
    return jnp.dot(a, b, preferred_element_type=F32)


def _inproj_kernel(x_ref, g_ref, w_ref, zf_ref, zb_ref, h_scr, *, nf):
    j = pl.program_id(1)

    @pl.when(j == 0)
    def _():
        h_scr[...] = _rms(x_ref[...], g_ref[...]).astype(BF16)

    @pl.when(j < nf)
    def _():
        zf_ref[...] = _dot(h_scr[...], w_ref[...])

    @pl.when(j >= nf)
    def _():
        zb_ref[...] = _dot(h_scr[...], w_ref[...]).astype(BF16)


def _inproj(x2, g, w, tm, tn):
    t = x2.shape[0]
    nf = ZF_WIDTH // tn
    return pl.pallas_call(
        functools.partial(_inproj_kernel, nf=nf),
        grid=(t // tm, (ZF_WIDTH + ZB_WIDTH) // tn),
        in_specs=[pl.BlockSpec((tm, D_MODEL), lambda i, j: (i, 0)),
                  pl.BlockSpec((1, D_MODEL), lambda i, j: (0, 0)),
                  pl.BlockSpec((D_MODEL, tn), lambda i, j: (0, j))],
        out_specs=[pl.BlockSpec((tm, tn), lambda i, j: (i, jnp.minimum(j, nf - 1))),
                   pl.BlockSpec((tm, tn), lambda i, j: (i, jnp.maximum(j - nf, 0)))],
        out_shape=[jax.ShapeDtypeStruct((t, ZF_WIDTH), F32),
                   jax.ShapeDtypeStruct((t, ZB_WIDTH), BF16)],
        scratch_shapes=[pltpu.VMEM((tm, D_MODEL), BF16)],
        compiler_params=_cparams(("parallel", "arbitrary")),
        name="inproj",
    )(x2, g, w)


def _dsa_prep_kernel(cq_ref, ckv_ref, sm_ref, qn_ref, kvn_ref, kig_ref, wuq_t_ref, wuqi_t_ref,
                     wuk_t_ref, qlat_ref, qidx_ref, widx_ref, kidx_ref, ckvn_ref, ckvt_ref):
    tm = cq_ref.shape[0]
    c_qt = _rms(cq_ref[...], qn_ref[0:1, :]).T.astype(BF16)
    q_t = _dot(wuq_t_ref[...], c_qt).astype(BF16)
    for h in range(A_HEADS):
        ql_t = (_dot(wuk_t_ref[h], q_t[h * A_HEAD_DIM:(h + 1) * A_HEAD_DIM, :])
                * (A_HEAD_DIM ** -0.5)).astype(BF16)
        qi_t = _dot(wuqi_t_ref[h], c_qt).astype(BF16)
        for j in range(tm // Q_BLOCK):
            cols = slice(j * Q_BLOCK, (j + 1) * Q_BLOCK)
            qlat_ref[j, :, h * Q_BLOCK:(h + 1) * Q_BLOCK] = ql_t[:, cols]
            qidx_ref[j, :, h * Q_BLOCK:(h + 1) * Q_BLOCK] = qi_t[:, cols]
    sm = sm_ref[...]
    w_t = sm.T[SM_W:SM_W + IDX_HEADS, :] * (IDX_HEADS ** -0.5 * IDX_DIM ** -0.5)
    for j in range(tm // Q_BLOCK):
        widx_ref[j] = w_t[:, j * Q_BLOCK:(j + 1) * Q_BLOCK]
    lane = lax.broadcasted_iota(I32, sm.shape, 1)
    inside = lane < IDX_DIM
    mu = jnp.sum(jnp.where(inside, sm, 0.0), axis=-1, keepdims=True) * (1.0 / IDX_DIM)
    xc = jnp.where(inside, sm - mu, 0.0)
    var = jnp.sum(xc * xc, axis=-1, keepdims=True) * (1.0 / IDX_DIM)
    kidx_ref[...] = (xc * lax.rsqrt(var + EPS) * kig_ref[0:1, :]).astype(BF16)
    ckvn = _rms(ckv_ref[...], kvn_ref[0:1, :])
    ckvn_ref[...] = ckvn.astype(BF16)
    ckvn_t = ckvn.T.astype(BF16)
    for j in range(tm // KEY_TILE):
        ckvt_ref[j] = ckvn_t[:, j * KEY_TILE:(j + 1) * KEY_TILE]


def _dsa_prep(z, qn, kvn, kig, wuq_t, wuqi_t, wuk_t, tm):
    t = z.shape[0]
    hq = A_HEADS * Q_BLOCK
    nqb = tm // Q_BLOCK
    qspec = pl.BlockSpec((nqb, LANES, hq), lambda i: (i, 0, 0))
    return pl.pallas_call(
        _dsa_prep_kernel,
        grid=(t // tm,),
        in_specs=[pl.BlockSpec((tm, Q_RANK), lambda i: (i, F_ACQ // Q_RANK)),
                  pl.BlockSpec((tm, KV_RANK), lambda i: (i, F_ACKV // KV_RANK)),
                  pl.BlockSpec((tm, LANES), lambda i: (i, F_SMALL // LANES)),
                  _resident((SUBLANES, Q_RANK)),
                  _resident((SUBLANES, KV_RANK)),
                  _resident((SUBLANES, LANES)),
                  _resident((A_WIDTH, Q_RANK)),
                  _resident((IDX_HEADS, LANES, Q_RANK)),
                  _resident((A_HEADS, KV_RANK, A_HEAD_DIM))],
        out_specs=[qspec, qspec,
                   pl.BlockSpec((nqb, IDX_HEADS, Q_BLOCK), lambda i: (i, 0, 0)),
                   pl.BlockSpec((tm, LANES), lambda i: (i, 0)),
                   pl.BlockSpec((tm, KV_RANK), lambda i: (i, 0)),
                   pl.BlockSpec((tm // KEY_TILE, KV_RANK, KEY_TILE), lambda i: (i, 0, 0))],
        out_shape=[jax.ShapeDtypeStruct((t // Q_BLOCK, KV_RANK, hq), BF16),
                   jax.ShapeDtypeStruct((t // Q_BLOCK, LANES, hq), BF16),
                   jax.ShapeDtypeStruct((t // Q_BLOCK, IDX_HEADS, Q_BLOCK), F32),
                   jax.ShapeDtypeStruct((t, LANES), BF16),
                   jax.ShapeDtypeStruct((t, KV_RANK), BF16),
                   jax.ShapeDtypeStruct((t // KEY_TILE, KV_RANK, KEY_TILE), BF16)],
        compiler_params=_cparams(("parallel",)),
        name="dsa_prep",
    )(z, z, z, qn, kvn, kig, wuq_t, wuqi_t, wuk_t)


def _fold(x, op, rows=SUBLANES):
    parts = [x[i:i + rows] for i in range(0, x.shape[0], rows)]
    while len(parts) > 1:
        parts = [op(parts[i], parts[i + 1]) for i in range(0, len(parts), 2)]
    return parts[0]


def _tile_loop(n, body, init, quads=False):
    def pair(j, carry):
        return body(j + 1, body(j, carry))

    def quad(j, carry):
        return pair(4 * j + 2, pair(4 * j, carry))

    start = 0
    carry = init
    if quads:
        carry = lax.fori_loop(0, n // 4, quad, carry)
        start = 4 * (n // 4)
    npair = (n - start) // 2
    carry = lax.fori_loop(0, npair, lambda j, c: pair(start + 2 * j, c), carry)
    return lax.fori_loop(start + 2 * npair, n, body, carry)


def _dsa_kernel(qidx_ref, qlat_ref, widx_ref, kidx_ref, ckv_ref, ckvt_ref, bias_ref, wuv_t_ref,
                out_ref, keys_scr, lg_scr, acc_scr, tie_scr, *, top_k, n_keys):
    QB = Q_BLOCK
    KT = KEY_TILE
    qblk = pl.program_id(1)
    nk = (qblk * QB + QB + KT - 1) // KT
    kf = float(top_k)
    hcols = [slice(h * QB, (h + 1) * QB) for h in range(A_HEADS)]
    hrows = [slice(h * SUBLANES, (h + 1) * SUBLANES) for h in range(A_HEADS)]

    q_pos = qblk * QB + lax.broadcasted_iota(I32, (KT, QB), 1)
    key_off = lax.broadcasted_iota(I32, (KT, QB), 0)

    qi_t = qidx_ref[0]
    w_rows = widx_ref[0]

    def score_body(kt, carry):
        s_t = _dot(kidx_ref[0, kt], qi_t)
        tot = jnp.zeros((KT, QB), F32)
        for g in range(IDX_HEADS):
            tot = tot + w_rows[g:g + 1, :] * jnp.maximum(s_t[:, hcols[g]], 0.0)
        key_pos = kt * KT + key_off
        tot = jnp.where(key_pos <= q_pos, tot + 0.0, -jnp.inf)
        bits = pltpu.bitcast(tot, I32)
        keys_scr[kt] = bits ^ ((bits >> 31) & 0x7FFFFFFF)
        return carry

    _tile_loop(nk, score_body, 0, quads=True)

    def count(pred):
        def body(kt, acc):
            return acc + _fold(jnp.where(pred(keys_scr[kt], kt), 1.0, 0.0), jnp.add)
        acc = _tile_loop(nk, body, jnp.zeros((SUBLANES, QB), F32))
        return jnp.sum(acc, axis=0, keepdims=True)

    def count_ge(cand):
        return count(lambda keys, kt: keys >= cand)

    n_bits = jnp.where((qblk + 1) * QB <= top_k, 0, 31)
    thr = jnp.where((count_ge(jnp.zeros((1, QB), I32)) >= kf) & (n_bits > 0), 0, INT_MIN).astype(I32)

    def bit_body(i, thr):
        cand = thr | (jnp.int32(1) << (30 - i))
        return jnp.where(count_ge(cand) >= kf, cand, thr)

    thr = lax.fori_loop(0, n_bits, bit_body, thr)

    tie_scr[...] = jnp.full((SUBLANES, QB), n_keys, I32)
    over = jnp.max(jnp.where(count_ge(thr) > kf, 1.0, 0.0))

    @pl.when(over > 0.0)
    def _():
        need = kf - count(lambda keys, kt: keys > thr)

        def idx_body(i, p):
            cand = p | (jnp.int32(1) << (n_keys.bit_length() - 1 - i))
            cnt = count(lambda keys, kt: (keys == thr) & (kt * KT + key_off < cand))
            return jnp.where(cnt < need, cand, p)

        p = lax.fori_loop(0, n_keys.bit_length(), idx_body, jnp.zeros((1, QB), I32))
        tie_scr[...] = jnp.broadcast_to(p, (SUBLANES, QB))

    tie_hi = tie_scr[0:1, :]

    ql_t = qlat_ref[0]

    def pass_a(kt, mx):
        lg_t = _dot(ckv_ref[0, kt], ql_t)
        keys = keys_scr[kt]
        key_pos = kt * KT + key_off
        sel = (keys > thr) | ((keys == thr) & (key_pos <= tie_hi))
        sel = sel & (key_pos <= q_pos)
        didx = jnp.minimum(qblk - (KT // QB) * kt, 3)
        new = []
        for h in range(A_HEADS):
            lh = jnp.where(sel, lg_t[:, hcols[h]] + bias_ref[didx, h], -jnp.inf)
            lg_scr[kt, :, hcols[h]] = lh
            new.append(jnp.maximum(mx[hrows[h]], _fold(lh, jnp.maximum)))
        return jnp.concatenate(new, axis=0)

    mx = _tile_loop(nk, pass_a, jnp.full((A_HEADS * SUBLANES, QB), -jnp.inf, F32), quads=True)
    m_rows = [jnp.max(mx[hrows[h]], axis=0, keepdims=True) for h in range(A_HEADS)]

    acc_scr[...] = jnp.zeros_like(acc_scr)

    def pass_b(kt, l):
        lg = lg_scr[kt]
        ps, new = [], []
        for h in range(A_HEADS):
            p = jnp.exp(lg[:, hcols[h]] - m_rows[h])
            new.append(l[hrows[h]] + _fold(p, jnp.add))
            ps.append(p.astype(BF16))
        acc_scr[...] = acc_scr[...] + _dot(ckvt_ref[0, kt], jnp.concatenate(ps, axis=1))
        return jnp.concatenate(new, axis=0)

    l = _tile_loop(nk, pass_b, jnp.zeros((A_HEADS * SUBLANES, QB), F32), quads=True)

    outs = []
    for h in range(A_HEADS):
        o_t = (acc_scr[:, hcols[h]] / jnp.sum(l[hrows[h]], axis=0, keepdims=True)).astype(BF16)
        outs.append(_dot(wuv_t_ref[h], o_t))
    out_ref[...] = jnp.concatenate(outs, axis=0).T.astype(BF16)


def _dsa(qidx_t, qlat_t, widx, kidx, ckv, ckv_t, bias_tab, wuv_t, b, s):
    nq = s // Q_BLOCK
    nt = s // KEY_TILE
    top_k = min(TOPK_MAX, s // 4)
    hq = A_HEADS * Q_BLOCK
    row = lambda bi, qi: bi * nq + qi
    qspec = pl.BlockSpec((1, LANES, hq), lambda bi, qi: (row(bi, qi), 0, 0))
    kspec = pl.BlockSpec((1, nt, KEY_TILE, LANES), lambda bi, qi: (bi, 0, 0, 0))
    return pl.pallas_call(
        functools.partial(_dsa_kernel, top_k=top_k, n_keys=s),
        grid=(b, nq),
        in_specs=[qspec, qspec,
                  pl.BlockSpec((1, IDX_HEADS, Q_BLOCK), lambda bi, qi: (row(bi, qi), 0, 0)),
                  kspec, kspec,
                  pl.BlockSpec((1, nt, KV_RANK, KEY_TILE), lambda bi, qi: (bi, 0, 0, 0)),
                  _resident((4, A_HEADS, KEY_TILE, Q_BLOCK)),
                  _resident((A_HEADS, A_HEAD_DIM, KV_RANK))],
        out_specs=pl.BlockSpec((Q_BLOCK, A_WIDTH), lambda bi, qi: (row(bi, qi), 0)),
        out_shape=jax.ShapeDtypeStruct((b * s, A_WIDTH), BF16),
        scratch_shapes=[pltpu.VMEM((nt, KEY_TILE, Q_BLOCK), I32),
                        pltpu.VMEM((nt, KEY_TILE, hq), F32),
                        pltpu.VMEM((KV_RANK, hq), F32),
                        pltpu.VMEM((SUBLANES, Q_BLOCK), I32)],
        compiler_params=_cparams(("parallel", "arbitrary")),
        name="dsa",
    )(qidx_t, qlat_t, widx, kidx.reshape(b, nt, KEY_TILE, LANES), ckv.reshape(b, nt, KEY_TILE, LANES),
      ckv_t.reshape(b, nt, KV_RANK, KEY_TILE), bias_tab, wuv_t)


FF_CHUNK = 256
TAIL_TILE = 512


def _mlstm_chunk(qk, sm, v_ref, o_ref, rows, gbias, ng_ref, consts, state):
    L = CHUNK
    causal, lower, upper, is_f, ones = consts
    c_st, m_st = state
    hs = range(M_HEADS)
    exact = dict(preferred_element_type=F32, precision=lax.Precision.HIGHEST)
    kt_all = qk[:, M_HEADS * M_QK:].T
    g = sm + gbias
    xg = jnp.where(is_f, jax.nn.log_sigmoid(g), g)
    rows_if = xg.T[SM_I:SM_I + 2 * M_HEADS, :]
    cum_r = jnp.dot(rows_if, upper, **exact)
    bc = jnp.dot(lower, xg, **exact)
    bc_q = [jnp.broadcast_to(bc[:, SM_F + h:SM_F + h + 1], (L, LANES)) for h in hs]
    b_r = [rows_if[h:h + 1, :] - cum_r[M_HEADS + h:M_HEADS + h + 1, :] for h in hs]
    b_last = [bc_q[h][L - 1:L, :] for h in hs]
    d_log = [jnp.where(causal, bc_q[h] + b_r[h], -jnp.inf) for h in hs]
    inter = [bc_q[h] + m_st[h] for h in hs]
    m_row = [jnp.maximum(inter[h], jnp.max(d_log[h], axis=-1, keepdims=True)) for h in hs]
    w_intra = [jnp.exp(d_log[h] - m_row[h]) for h in hs]
    w_inter = [jnp.exp(inter[h] - m_row[h]) for h in hs]
    floor = [jnp.exp(-m_row[h]) for h in hs]

    qb = [(qk[:, h * M_QK:(h + 1) * M_QK] * (M_QK ** -0.5)).astype(BF16) for h in hs]
    kt = [kt_all[h * M_QK:(h + 1) * M_QK, :] for h in hs]
    vaug = [jnp.concatenate([v_ref[rows, h * M_V:(h + 1) * M_V], ones], axis=1) for h in hs]
    sc = [_dot(qb[h], kt[h].astype(BF16)) for h in hs]
    cross = [_dot(qb[h], c_st[h].astype(BF16)) for h in hs]

    g_r = [b_last[h] + b_r[h] for h in hs]
    m_new = [jnp.maximum(b_last[h] + m_st[h], jnp.max(g_r[h], axis=-1, keepdims=True)) for h in hs]
    wk_r = [jnp.exp(g_r[h] - m_new[h]) for h in hs]
    decay = [jnp.exp(b_last[h] + m_st[h] - m_new[h]) for h in hs]
    upd = [_dot((kt[h] * wk_r[h]).astype(BF16), vaug[h]) for h in hs]

    sc = [(sc[h] * w_intra[h]).astype(BF16) for h in hs]
    num = [_dot(sc[h], vaug[h]) + jnp.concatenate([w_inter[h], w_inter[h]], axis=1) * cross[h] for h in hs]
    out = [num[h][:, :M_V] / jnp.maximum(jnp.abs(num[h][:, M_V:]), floor[h]) for h in hs]
    c_new = [jnp.concatenate([decay[h], decay[h]], axis=1) * c_st[h] + upd[h] for h in hs]
    y = [_group_norm(out[h], ng_ref[0:1, h * M_V:(h + 1) * M_V]) for h in hs]
    hm = [(y[h] * jax.nn.sigmoid(o_ref[rows, h * M_V:(h + 1) * M_V].astype(F32))).astype(BF16) for h in hs]
    return jnp.concatenate(hm, axis=1), (c_new, m_new)


def _ret_chunk(q_ref, k_ref, v_ref, g_ref, rows, cosf, sinf, intra_ref, qdec_ref, kdec_ref, cdec_ref,
               ng_ref, state):
    hs = range(R_HEADS)
    sl = [slice(h * R_QK, (h + 1) * R_QK) for h in hs]
    half = R_QK // 2

    def rot(x):
        return x * cosf + pltpu.roll(x, half, 1) * sinf

    qr = [rot(q_ref[rows, sl[h]]).astype(BF16) for h in hs]
    kt = [(rot(k_ref[rows, sl[h]]) * (R_QK ** -0.5)).T for h in hs]
    vb = [v_ref[rows, sl[h]] for h in hs]
    sc = [_dot(qr[h], kt[h].astype(BF16)) for h in hs]
    cross = [_dot(qr[h], state[h].astype(BF16)) for h in hs]
    upd = [_dot((kt[h] * kdec_ref[h:h + 1, :]).astype(BF16), vb[h]) for h in hs]
    sc = [(sc[h] * intra_ref[h]).astype(BF16) for h in hs]
    out = [_dot(sc[h], vb[h]) + qdec_ref[h] * cross[h] for h in hs]
    new = [cdec_ref[h:h + 1, :] * state[h] + upd[h] for h in hs]
    y = [_group_norm(out[h], ng_ref[0:1, sl[h]]) for h in hs]
    hr = []
    for h in hs:
        gt = g_ref[rows, sl[h]].astype(F32)
        hr.append((y[h] * (gt * jax.nn.sigmoid(gt))).astype(BF16))
    return jnp.concatenate(hr, axis=1), new


def _tail_kernel(x_ref, ha_ref, gm_ref, ga_ref, gr_ref,
                 qk_ref, sm_ref, mv_ref, mo_ref, rq_ref, rk_ref, rv_ref, rg_ref, cos_ref, sin_ref,
                 conv_ref, gb_ref, mng_ref, intra_ref, qdec_ref, kdec_ref, cdec_ref, rng_ref,
                 pm_ref, pa_ref, pr_ref, wo_ref, n2_ref, wg_ref, wu_ref, wd_ref, fg_ref,
                 out_ref, hm_scr, hr_scr, xbuf, cst, mst, rst, *, final, n_tiles, tiles_per_row):
    L = CHUNK
    TM = TAIL_TILE
    i = pl.program_id(0)
    scan_tile = jnp.minimum(i, n_tiles - 1)
    cur = lax.rem(i, 2)
    prev = lax.rem(i + 1, 2)

    @pl.when(i == 0)
    def _():
        hm_scr[...] = jnp.zeros_like(hm_scr)
        hr_scr[...] = jnp.zeros_like(hr_scr)

    @pl.when(lax.rem(scan_tile, tiles_per_row) == 0)
    def _():
        xbuf[0:SUBLANES, :] = jnp.zeros((SUBLANES, 2 * M_HEADS * M_QK), F32)
        cst[...] = jnp.zeros_like(cst)
        mst[...] = jnp.zeros_like(mst)
        rst[...] = jnp.zeros_like(rst)

    y = (jax.nn.sigmoid(gm_ref[...].astype(F32)) * _dot(hm_scr[prev], pm_ref[...])
         + jax.nn.sigmoid(ga_ref[...].astype(F32)) * _dot(ha_ref[...], pa_ref[...])
         + jax.nn.sigmoid(gr_ref[...].astype(F32)) * _dot(hr_scr[prev], pr_ref[...]))
    x1 = x_ref[...] + _dot(y.astype(BF16), wo_ref[...])
    h2 = _rms(x1, n2_ref[0:1, :]).astype(BF16)

    x_qk = qk_ref[...]
    xbuf[SUBLANES:SUBLANES + TM, :] = x_qk
    w = conv_ref[...]
    conv = w[CONV_W - 1:CONV_W, :] * x_qk
    for j in range(CONV_W - 1):
        off = SUBLANES - (CONV_W - 1) + j
        conv = conv + w[j:j + 1, :] * xbuf[off:off + TM, :]
    xbuf[0:SUBLANES, :] = x_qk[TM - SUBLANES:TM, :]
    qk_all = conv * jax.nn.sigmoid(conv)

    r_i = lax.broadcasted_iota(I32, (L, L), 0)
    c_i = lax.broadcasted_iota(I32, (L, L), 1)
    lane = lax.broadcasted_iota(I32, (L, LANES), 1)
    causal = r_i >= c_i
    consts = (causal, causal.astype(F32), (r_i <= c_i).astype(F32),
              (lane >= SM_F) & (lane < SM_F + M_HEADS), jnp.ones((L, M_V), BF16))
    gbias = gb_ref[0:1, :]
    m_state = ([cst[h] for h in range(M_HEADS)], [mst[h:h + 1, :] for h in range(M_HEADS)])
    r_state = [rst[h] for h in range(R_HEADS)]

    def scan_unit(u, m_state, r_state):
        c = u // 2
        rows = slice(c * L, (c + 1) * L)
        if u % 2 == 0:
            hm, m_state = _mlstm_chunk(qk_all[rows], sm_ref[rows, :], mv_ref, mo_ref, rows, gbias,
                                       mng_ref, consts, m_state)
            hm_scr[cur, rows, :] = hm
        else:
            hr, r_state = _ret_chunk(rq_ref, rk_ref, rv_ref, rg_ref, rows, cos_ref[rows, :],
                                     sin_ref[rows, :], intra_ref, qdec_ref, kdec_ref, cdec_ref,
                                     rng_ref, r_state)
            hr_scr[cur, rows, :] = hr
        return m_state, r_state

    n_units = 2 * (TM // L)
    n_ff = D_FF // FF_CHUNK
    assert n_units <= n_ff
    acc = x1
    for c in range(n_ff):
        cs = slice(c * FF_CHUNK, (c + 1) * FF_CHUNK)
        gate = _dot(h2, wg_ref[:, cs])
        up = _dot(h2, wu_ref[:, cs])
        act = (gate * jax.nn.sigmoid(gate) * up).astype(BF16)
        acc = acc + _dot(act, wd_ref[cs, :])
        if c < n_units:
            m_state, r_state = scan_unit(c, m_state, r_state)

    for h in range(M_HEADS):
        cst[h] = m_state[0][h]
    mst[0:M_HEADS, :] = jnp.concatenate(m_state[1], axis=0)
    for h in range(R_HEADS):
        rst[h] = r_state[h]
    if final:
        acc = _rms(acc, fg_ref[0:1, :])
    out_ref[...] = acc


def _tail(x2, ha, zf, zb, cosf, sinf, conv_w, gbias, mng, intra, qdec, kdec, cdec, rng,
          pm, pa, pr, wo, n2, wg, wu, wd, fg, s, final):
    t = x2.shape[0]
    tm = TAIL_TILE
    n_tiles = t // tm
    tiles_per_row = s // tm
    w512 = 2 * M_HEADS * M_QK
    merge_tile = lambda i: jnp.maximum(i - 1, 0)
    scan_tile = lambda i: jnp.minimum(i, n_tiles - 1)
    mspec = lambda w, cb=0: pl.BlockSpec((tm, w), lambda i: (merge_tile(i), cb))
    sspec = lambda w, off: pl.BlockSpec((tm, w), lambda i: (scan_tile(i), off // w))
    pspec = pl.BlockSpec((tm, R_QK), lambda i: (lax.rem(scan_tile(i), tiles_per_row), 0))
    return pl.pallas_call(
        functools.partial(_tail_kernel, final=final, n_tiles=n_tiles, tiles_per_row=tiles_per_row),
        grid=(n_tiles + 1,),
        in_specs=[mspec(D_MODEL), mspec(A_WIDTH),
                  mspec(D_MODEL, B_GM // D_MODEL), mspec(D_MODEL, B_GA // D_MODEL),
                  mspec(D_MODEL, B_GR // D_MODEL),
                  sspec(w512, F_MQK), sspec(LANES, F_SMALL), sspec(M_WIDTH, B_MV), sspec(M_WIDTH, B_MO),
                  sspec(R_WIDTH, F_RQ), sspec(R_WIDTH, F_RK), sspec(R_WIDTH, B_RV), sspec(R_WIDTH, B_RG),
                  pspec, pspec,
                  _resident((SUBLANES, w512)), _resident((SUBLANES, LANES)), _resident((SUBLANES, M_WIDTH)),
                  _resident((R_HEADS, CHUNK, CHUNK)), _resident((R_HEADS, CHUNK, R_V)),
                  _resident((SUBLANES, CHUNK)), _resident((SUBLANES, R_V)), _resident((SUBLANES, R_WIDTH)),
                  _resident((M_WIDTH, D_MODEL)), _resident((A_WIDTH, D_MODEL)),
                  _resident((R_WIDTH, D_MODEL)), _resident((D_MODEL, D_MODEL)),
                  _resident((SUBLANES, D_MODEL)),
                  _resident((D_MODEL, D_FF)), _resident((D_MODEL, D_FF)),
                  _resident((D_FF, D_MODEL)), _resident((SUBLANES, D_MODEL))],
        out_specs=mspec(D_MODEL),
        out_shape=jax.ShapeDtypeStruct((t, D_MODEL), F32),
        scratch_shapes=[pltpu.VMEM((2, tm, M_WIDTH), BF16),
                        pltpu.VMEM((2, tm, R_WIDTH), BF16),
                        pltpu.VMEM((SUBLANES + tm, w512), F32),
                        pltpu.VMEM((M_HEADS, M_QK, 2 * M_V), F32),
                        pltpu.VMEM((SUBLANES, LANES), F32),
                        pltpu.VMEM((R_HEADS, R_QK, R_V), F32)],
        compiler_params=_cparams(("arbitrary",), VMEM_LIMIT_TAIL),
        name="tail",
    )(x2, ha, zb, zb, zb, zf, zf, zb, zb, zf, zf, zb, zb, cosf, sinf,
      conv_w, gbias, mng, intra, qdec, kdec, cdec, rng, pm, pa, pr, wo, n2, wg, wu, wd, fg)


def _pad_rows(v, rows=SUBLANES):
    v = jnp.atleast_2d(v).astype(F32)
    return jnp.pad(v, ((0, rows - v.shape[0]), (0, 0)))


def _t5_bucket(dist):
    max_exact = N_BUCKETS // 2
    d_f = jnp.maximum(dist, 1).astype(F32)
    large = max_exact + (jnp.log(d_f / max_exact) / math.log(MAX_DISTANCE / max_exact)
                         * (N_BUCKETS - max_exact)).astype(I32)
    large = jnp.minimum(large, N_BUCKETS - 1)
    return jnp.where(dist < max_exact, dist, large)


def _bias_table(rel_bias):
    d = jnp.arange(4, dtype=I32)[:, None, None]
    j = jnp.arange(KEY_TILE, dtype=I32)[None, :, None]
    i = jnp.arange(Q_BLOCK, dtype=I32)[None, None, :]
    bucket = _t5_bucket(jnp.maximum(i + d * Q_BLOCK - j, 0))
    onehot = (bucket[None] == jnp.arange(N_BUCKETS, dtype=I32)[:, None, None, None]).astype(F32)
    return jnp.einsum("nh,ndkq->dhkq", rel_bias.astype(F32), onehot, precision=lax.Precision.HIGHEST)


def _rotary_tables(s):
    half = R_QK // 2
    freqs = ROPE_BASE ** (-jnp.linspace(0.0, 1.0, half, dtype=F32))
    ang = jnp.arange(s, dtype=F32)[:, None] * freqs[None, :]
    cos, sin = jnp.cos(ang), jnp.sin(ang)
    return jnp.concatenate([cos, cos], axis=1), jnp.concatenate([-sin, sin], axis=1)


def _retention_tables():
    h = R_HEADS
    log_gamma = jnp.log1p(-jnp.exp2(-5.0 - jnp.arange(h, dtype=F32)))
    pos = jnp.arange(CHUNK, dtype=F32)
    diff = pos[:, None] - pos[None, :]
    intra = jnp.where(diff >= 0, jnp.exp(jnp.maximum(diff, 0.0)[None] * log_gamma[:, None, None]), 0.0)
    q_decay = jnp.exp((pos[None, :] + 1.0) * log_gamma[:, None])
    k_decay = jnp.exp((CHUNK - 1.0 - pos[None, :]) * log_gamma[:, None])
    chunk_decay = jnp.exp(CHUNK * log_gamma)
    qdec = jnp.broadcast_to(q_decay[:, :, None], (h, CHUNK, R_V))
    kdec = _pad_rows(k_decay)
    cdec = _pad_rows(jnp.broadcast_to(chunk_decay[:, None], (h, R_V)))
    return intra, qdec, kdec, cdec


def _layout_w_in(w_in):
    parts = jnp.split(w_in, [int(c) for c in np.cumsum(IN_SPLITS)[:-1]], axis=-1)
    (m_q, m_k, m_v, m_i, m_f, m_o, a_cq, a_kidx, a_widx, a_ckv,
     r_q, r_k, r_v, r_g, g_m, g_a, g_r) = parts
    small = jnp.concatenate([a_kidx, m_i, m_f, a_widx], axis=-1)
    small = jnp.pad(small, ((0, 0), (0, LANES - small.shape[-1])))
    w = jnp.concatenate([m_q, m_k, a_cq, a_ckv, small, r_q, r_k,
                         m_v, m_o, r_v, r_g, g_m, g_a, g_r], axis=-1)
    assert w.shape[-1] == ZF_WIDTH + ZB_WIDTH
    return w.astype(BF16)


def _row_tile(t, want):
    while t % want:
        want //= 2
    return want


def kernel(x, rel_bias, final_norm_g, norm1_g, w_in, m_conv, m_ibias, m_fbias, m_norm_g, a_qnorm_g, a_wuq, a_wuq_idx, a_kidx_g, a_kvnorm_g, a_wuk, a_wuv, r_norm_g, p_m, p_a, p_r, w_out, norm2_g, w_gate, w_up, w_down):
    b, s, d = x.shape
    depth = w_in.shape[0]
    assert d == D_MODEL and s % KEY_TILE == 0 and s % TAIL_TILE == 0
    assert SM_F == SM_I + M_HEADS and M_V == LANES and R_V == LANES and CHUNK == LANES
    t = b * s
    tm_in = _row_tile(t, 2048)
    tm = _row_tile(t, 512)

    bias_tab = _bias_table(rel_bias)
    cosf, sinf = _rotary_tables(s)
    intra, qdec, kdec, cdec = _retention_tables()
    fg = _pad_rows(final_norm_g)

    x2 = x.reshape(t, d)
    for l in range(depth):
        zf, zb = _inproj(x2, norm1_g[l][None, :], _layout_w_in(w_in[l]), tm_in, 1024)

        gbias = jnp.zeros((LANES,), F32)
        gbias = gbias.at[SM_I:SM_I + M_HEADS].set(m_ibias[l]).at[SM_F:SM_F + M_HEADS].set(m_fbias[l])

        wuqi_t = a_wuq_idx[l].reshape(Q_RANK, IDX_HEADS, IDX_DIM).transpose(1, 2, 0)
        wuqi_t = jnp.pad(wuqi_t, ((0, 0), (0, LANES - IDX_DIM), (0, 0))).astype(BF16)
        kig = jnp.pad(a_kidx_g[l], (0, LANES - IDX_DIM))
        qlat_t, qidx_t, widx, kidx, ckv, ckv_t = _dsa_prep(
            zf, _pad_rows(a_qnorm_g[l]), _pad_rows(a_kvnorm_g[l]), _pad_rows(kig),
            a_wuq[l].T.astype(BF16), wuqi_t, jnp.swapaxes(a_wuk[l], 1, 2).astype(BF16), tm)
        ha = _dsa(qidx_t, qlat_t, widx, kidx, ckv, ckv_t, bias_tab,
                  jnp.swapaxes(a_wuv[l], 1, 2).astype(BF16), b, s)

        x2 = _tail(x2, ha, zf, zb, cosf, sinf, _pad_rows(m_conv[l]), _pad_rows(gbias),
                   _pad_rows(m_norm_g[l]), intra, qdec, kdec, cdec, _pad_rows(r_norm_g[l]),
                   p_m[l].astype(BF16), p_a[l].astype(BF16), p_r[l].astype(BF16),
                   w_out[l].astype(BF16), _pad_rows(norm2_g[l]), w_gate[l].astype(BF16),
                   w_up[l].astype(BF16), w_down[l].astype(BF16), fg, s, final=(l == depth - 1))
    return x2.reshape(b, s, d)
```

```python
import functools
import math

import numpy as np
import jax
import jax.numpy as jnp
from jax import lax
from jax.experimental import pallas as pl
from jax.experimental.pallas import tpu as pltpu

F32 = jnp.float32
BF16 = jnp.bfloat16
I32 = jnp.int32

D_MODEL = 1024
M_HEADS, M_QK, M_V = 4, 64, 128
M_WIDTH = M_HEADS * M_V
CONV_W = 4
A_HEADS, A_HEAD_DIM = 8, 64
A_WIDTH = A_HEADS * A_HEAD_DIM
Q_RANK, KV_RANK = 256, 128
IDX_HEADS, IDX_DIM = 8, 64
TOPK_MAX = 256
R_HEADS, R_QK, R_V = 4, 128, 128
R_WIDTH = R_HEADS * R_V
CHUNK = 128
Q_BLOCK = 256
N_BUCKETS = 32
MAX_DISTANCE = 128
ROPE_BASE = 10000.0
EPS = 1e-6
D_FF = -(((-8 * D_MODEL) // 3) // 256) * 256
IN_SPLITS = (M_HEADS * M_QK, M_HEADS * M_QK, M_WIDTH, M_HEADS, M_HEADS, M_WIDTH,
             Q_RANK, IDX_DIM, IDX_HEADS, KV_RANK,
             R_HEADS * R_QK, R_HEADS * R_QK, R_WIDTH, R_WIDTH,
             D_MODEL, D_MODEL, D_MODEL)

LANES = 128
SUBLANES = 8
VMEM_LIMIT = 56 * 1024 * 1024
VMEM_LIMIT_TAIL = 62 * 1024 * 1024

F_MQK = 0
F_ACQ = 512
F_ACKV = 768
F_SMALL = 896
F_RQ, F_RK = 1024, 1536
ZF_WIDTH = 2048
B_MV, B_MO, B_RV, B_RG = 0, 512, 1024, 1536
B_GM, B_GA, B_GR = 2048, 3072, 4096
ZB_WIDTH = 5120
SM_KIDX, SM_I, SM_F, SM_W = 0, 64, 68, 72

KEY_TILE = 256
N_BIAS_TABLES = -(-(MAX_DISTANCE + KEY_TILE - 1) // Q_BLOCK) + 1
INT_MIN = -2 ** 31


def _cparams(sem, vmem_limit=VMEM_LIMIT):
    return pltpu.CompilerParams(dimension_semantics=sem, vmem_limit_bytes=vmem_limit)


def _resident(shape):
    nd = len(shape)
    return pl.BlockSpec(shape, lambda *_: (0,) * nd, pipeline_mode=pl.Buffered(1))


def _rms(x, g):
    return x * lax.rsqrt(jnp.mean(x * x, axis=-1, keepdims=True) + EPS) * g


def _group_norm(x, g):
    mu = jnp.mean(x, axis=-1, keepdims=True)
    xc = x - mu
    return xc * lax.rsqrt(jnp.mean(xc * xc, axis=-1, keepdims=True) + EPS) * g


def _dot(a, b):
    return jnp.dot(a, b, preferred_element_type=F32)


def _inproj_kernel(x_ref, g_ref, w_ref, zf_ref, zb_ref, h_scr, *, nf):
    j = pl.program_id(1)

    @pl.when(j == 0)
    def _():
        h_scr[...] = _rms(x_ref[...], g_ref[...]).astype(BF16)

    @pl.when(j < nf)
    def _():
        zf_ref[...] = _dot(h_scr[...], w_ref[...])

    @pl.when(j >= nf)
    def _():
        zb_ref[...] = _dot(h_scr[...], w_ref[...]).astype(BF16)


def _inproj(x2, g, w, tm, tn):
    t = x2.shape[0]
    nf = ZF_WIDTH // tn
    return pl.pallas_call(
        functools.partial(_inproj_kernel, nf=nf),
        grid=(t // tm, (ZF_WIDTH + ZB_WIDTH) // tn),
        in_specs=[pl.BlockSpec((tm, D_MODEL), lambda i, j: (i, 0)),
                  pl.BlockSpec((1, D_MODEL), lambda i, j: (0, 0)),
                  pl.BlockSpec((D_MODEL, tn), lambda i, j: (0, j))],
        out_specs=[pl.BlockSpec((tm, tn), lambda i, j: (i, jnp.minimum(j, nf - 1))),
                   pl.BlockSpec((tm, tn), lambda i, j: (i, jnp.maximum(j - nf, 0)))],
        out_shape=[jax.ShapeDtypeStruct((t, ZF_WIDTH), F32),
                   jax.ShapeDtypeStruct((t, ZB_WIDTH), BF16)],
        scratch_shapes=[pltpu.VMEM((tm, D_MODEL), BF16)],
        compiler_params=_cparams(("parallel", "arbitrary")),
        name="inproj",
    )(x2, g, w)


def _dsa_prep_kernel(cq_ref, ckv_ref, sm_ref, qn_ref, kvn_ref, kig_ref, wuq_t_ref, wuqi_t_ref,
                     wuk_t_ref, qlat_ref, qidx_ref, widx_ref, kidx_ref, ckvn_ref, ckvt_ref):
    tm = cq_ref.shape[0]
    c_qt = _rms(cq_ref[...], qn_ref[0:1, :]).T.astype(BF16)
    q_t = _dot(wuq_t_ref[...], c_qt).astype(BF16)
    for h in range(A_HEADS):
        ql_t = (_dot(wuk_t_ref[h], q_t[h * A_HEAD_DIM:(h + 1) * A_HEAD_DIM, :])
                * (A_HEAD_DIM ** -0.5)).astype(BF16)
        qi_t = _dot(wuqi_t_ref[h], c_qt).astype(BF16)
        for j in range(tm // Q_BLOCK):
            cols = slice(j * Q_BLOCK, (j + 1) * Q_BLOCK)
            qlat_ref[j, :, h * Q_BLOCK:(h + 1) * Q_BLOCK] = ql_t[:, cols]
            qidx_ref[j, :, h * Q_BLOCK:(h + 1) * Q_BLOCK] = qi_t[:, cols]
    sm = sm_ref[...]
    w_t = sm.T[SM_W:SM_W + IDX_HEADS, :] * (IDX_HEADS ** -0.5 * IDX_DIM ** -0.5)
    for j in range(tm // Q_BLOCK):
        widx_ref[j] = w_t[:, j * Q_BLOCK:(j + 1) * Q_BLOCK]
    lane = lax.broadcasted_iota(I32, sm.shape, 1)
    inside = lane < IDX_DIM
    mu = jnp.sum(jnp.where(inside, sm, 0.0), axis=-1, keepdims=True) * (1.0 / IDX_DIM)
    xc = jnp.where(inside, sm - mu, 0.0)
    var = jnp.sum(xc * xc, axis=-1, keepdims=True) * (1.0 / IDX_DIM)
    kidx_ref[...] = (xc * lax.rsqrt(var + EPS) * kig_ref[0:1, :]).astype(BF16)
    ckvn = _rms(ckv_ref[...], kvn_ref[0:1, :])
    ckvn_ref[...] = ckvn.astype(BF16)
    ckvn_t = ckvn.T.astype(BF16)
    for j in range(tm // KEY_TILE):
        ckvt_ref[j] = ckvn_t[:, j * KEY_TILE:(j + 1) * KEY_TILE]


def _dsa_prep(z, qn, kvn, kig, wuq_t, wuqi_t, wuk_t, tm):
    t = z.shape[0]
    hq = A_HEADS * Q_BLOCK
    nqb = tm // Q_BLOCK
    qspec = pl.BlockSpec((nqb, LANES, hq), lambda i: (i, 0, 0))
    return pl.pallas_call(
        _dsa_prep_kernel,
        grid=(t // tm,),
        in_specs=[pl.BlockSpec((tm, Q_RANK), lambda i: (i, F_ACQ // Q_RANK)),
                  pl.BlockSpec((tm, KV_RANK), lambda i: (i, F_ACKV // KV_RANK)),
                  pl.BlockSpec((tm, LANES), lambda i: (i, F_SMALL // LANES)),
                  _resident((SUBLANES, Q_RANK)),
                  _resident((SUBLANES, KV_RANK)),
                  _resident((SUBLANES, LANES)),
                  _resident((A_WIDTH, Q_RANK)),
                  _resident((IDX_HEADS, LANES, Q_RANK)),
                  _resident((A_HEADS, KV_RANK, A_HEAD_DIM))],
        out_specs=[qspec, qspec,
                   pl.BlockSpec((nqb, IDX_HEADS, Q_BLOCK), lambda i: (i, 0, 0)),
                   pl.BlockSpec((tm, LANES), lambda i: (i, 0)),
                   pl.BlockSpec((tm, KV_RANK), lambda i: (i, 0)),
                   pl.BlockSpec((tm // KEY_TILE, KV_RANK, KEY_TILE), lambda i: (i, 0, 0))],
        out_shape=[jax.ShapeDtypeStruct((t // Q_BLOCK, KV_RANK, hq), BF16),
                   jax.ShapeDtypeStruct((t // Q_BLOCK, LANES, hq), BF16),
                   jax.ShapeDtypeStruct((t // Q_BLOCK, IDX_HEADS, Q_BLOCK), F32),
                   jax.ShapeDtypeStruct((t, LANES), BF16),
                   jax.ShapeDtypeStruct((t, KV_RANK), BF16),
                   jax.ShapeDtypeStruct((t // KEY_TILE, KV_RANK, KEY_TILE), BF16)],
        compiler_params=_cparams(("parallel",)),
        name="dsa_prep",
    )(z, z, z, qn, kvn, kig, wuq_t, wuqi_t, wuk_t)


def _fold(x, op, rows=SUBLANES):
    parts = [x[i:i + rows] for i in range(0, x.shape[0], rows)]
    while len(parts) > 1:
        parts = [op(parts[i], parts[i + 1]) for i in range(0, len(parts), 2)]
    return parts[0]


def _tile_loop(n, body, init, quads=False):
    def pair(j, carry):
        return body(j + 1, body(j, carry))

    def quad(j, carry):
        return pair(4 * j + 2, pair(4 * j, carry))

    start = 0
    carry = init
    if quads:
        carry = lax.fori_loop(0, n // 4, quad, carry)
        start = 4 * (n // 4)
    npair = (n - start) // 2
    carry = lax.fori_loop(0, npair, lambda j, c: pair(start + 2 * j, c), carry)
    return lax.fori_loop(start + 2 * npair, n, body, carry)


def _dsa_kernel(qidx_ref, qlat_ref, widx_ref, kidx_ref, ckv_ref, ckvt_ref, bias_ref, wuv_t_ref,
                out_ref, keys_scr, lg_scr, acc_scr, tie_scr, *, top_k, n_keys):
    QB = Q_BLOCK
    KT = KEY_TILE
    qblk = pl.program_id(1)
    nk = (qblk * QB + QB + KT - 1) // KT
    kf = float(top_k)
    hcols = [slice(h * QB, (h + 1) * QB) for h in range(A_HEADS)]
    hrows = [slice(h * SUBLANES, (h + 1) * SUBLANES) for h in range(A_HEADS)]

    q_pos = qblk * QB + lax.broadcasted_iota(I32, (KT, QB), 1)
    key_off = lax.broadcasted_iota(I32, (KT, QB), 0)

    qi_t = qidx_ref[0]
    w_rows = widx_ref[0]

    def score_body(kt, carry):
        s_t = _dot(kidx_ref[0, kt], qi_t)
        tot = jnp.zeros((KT, QB), F32)
        for g in range(IDX_HEADS):
            tot = tot + w_rows[g:g + 1, :] * jnp.maximum(s_t[:, hcols[g]], 0.0)
        key_pos = kt * KT + key_off
        tot = jnp.where(key_pos <= q_pos, tot + 0.0, -jnp.inf)
        bits = pltpu.bitcast(tot, I32)
        keys_scr[kt] = bits ^ ((bits >> 31) & 0x7FFFFFFF)
        return carry

    _tile_loop(nk, score_body, 0, quads=True)

    def count(pred):
        def body(kt, acc):
            return acc + _fold(jnp.where(pred(keys_scr[kt], kt), 1.0, 0.0), jnp.add)
        acc = _tile_loop(nk, body, jnp.zeros((SUBLANES, QB), F32))
        return jnp.sum(acc, axis=0, keepdims=True)

    def count_ge(cand):
        return count(lambda keys, kt: keys >= cand)

    n_bits = jnp.where((qblk + 1) * QB <= top_k, 0, 31)
    thr = jnp.where((count_ge(jnp.zeros((1, QB), I32)) >= kf) & (n_bits > 0), 0, INT_MIN).astype(I32)

    def bit_body(i, thr):
        cand = thr | (jnp.int32(1) << (30 - i))
        return jnp.where(count_ge(cand) >= kf, cand, thr)

    thr = lax.fori_loop(0, n_bits, bit_body, thr)

    tie_scr[...] = jnp.full((SUBLANES, QB), n_keys, I32)
    over = jnp.max(jnp.where(count_ge(thr) > kf, 1.0, 0.0))

    @pl.when(over > 0.0)
    def _():
        need = kf - count(lambda keys, kt: keys > thr)

        def idx_body(i, p):
            cand = p | (jnp.int32(1) << (n_keys.bit_length() - 1 - i))
            cnt = count(lambda keys, kt: (keys == thr) & (kt * KT + key_off < cand))
            return jnp.where(cnt < need, cand, p)

        p = lax.fori_loop(0, n_keys.bit_length(), idx_body, jnp.zeros((1, QB), I32))
        tie_scr[...] = jnp.broadcast_to(p, (SUBLANES, QB))

    tie_hi = tie_scr[0:1, :]

    ql_t = qlat_ref[0]

    def pass_a(kt, mx):
        lg_t = _dot(ckv_ref[0, kt], ql_t)
        keys = keys_scr[kt]
        key_pos = kt * KT + key_off
        sel = (keys > thr) | ((keys == thr) & (key_pos <= tie_hi))
        sel = sel & (key_pos <= q_pos)
        didx = jnp.minimum((qblk * QB - kt * KT) // QB, N_BIAS_TABLES - 1)
        new = []
        for h in range(A_HEADS):
            lh = jnp.where(sel, lg_t[:, hcols[h]] + bias_ref[didx, h], -jnp.inf)
            lg_scr[kt, :, hcols[h]] = lh
            new.append(jnp.maximum(mx[hrows[h]], _fold(lh, jnp.maximum)))
        return jnp.concatenate(new, axis=0)

    mx = _tile_loop(nk, pass_a, jnp.full((A_HEADS * SUBLANES, QB), -jnp.inf, F32), quads=True)
    m_rows = [jnp.max(mx[hrows[h]], axis=0, keepdims=True) for h in range(A_HEADS)]

    acc_scr[...] = jnp.zeros_like(acc_scr)

    def pass_b(kt, l):
        lg = lg_scr[kt]
        ps, new = [], []
        for h in range(A_HEADS):
            p = jnp.exp(lg[:, hcols[h]] - m_rows[h])
            new.append(l[hrows[h]] + _fold(p, jnp.add))
            ps.append(p.astype(BF16))
        acc_scr[...] = acc_scr[...] + _dot(ckvt_ref[0, kt], jnp.concatenate(ps, axis=1))
        return jnp.concatenate(new, axis=0)

    l = _tile_loop(nk, pass_b, jnp.zeros((A_HEADS * SUBLANES, QB), F32), quads=True)

    outs = []
    for h in range(A_HEADS):
        o_t = (acc_scr[:, hcols[h]] / jnp.sum(l[hrows[h]], axis=0, keepdims=True)).astype(BF16)
        outs.append(_dot(wuv_t_ref[h], o_t))
    out_ref[...] = jnp.concatenate(outs, axis=0).T.astype(BF16)


def _dsa(qidx_t, qlat_t, widx, kidx, ckv, ckv_t, bias_tab, wuv_t, b, s):
    nq = s // Q_BLOCK
    nt = s // KEY_TILE
    top_k = min(TOPK_MAX, s // 4)
    hq = A_HEADS * Q_BLOCK
    row = lambda bi, qi: bi * nq + qi
    qspec = pl.BlockSpec((1, LANES, hq), lambda bi, qi: (row(bi, qi), 0, 0))
    kspec = pl.BlockSpec((1, nt, KEY_TILE, LANES), lambda bi, qi: (bi, 0, 0, 0))
    return pl.pallas_call(
        functools.partial(_dsa_kernel, top_k=top_k, n_keys=s),
        grid=(b, nq),
        in_specs=[qspec, qspec,
                  pl.BlockSpec((1, IDX_HEADS, Q_BLOCK), lambda bi, qi: (row(bi, qi), 0, 0)),
                  kspec, kspec,
                  pl.BlockSpec((1, nt, KV_RANK, KEY_TILE), lambda bi, qi: (bi, 0, 0, 0)),
                  _resident((N_BIAS_TABLES, A_HEADS, KEY_TILE, Q_BLOCK)),
                  _resident((A_HEADS, A_HEAD_DIM, KV_RANK))],
        out_specs=pl.BlockSpec((Q_BLOCK, A_WIDTH), lambda bi, qi: (row(bi, qi), 0)),
        out_shape=jax.ShapeDtypeStruct((b * s, A_WIDTH), BF16),
        scratch_shapes=[pltpu.VMEM((nt, KEY_TILE, Q_BLOCK), I32),
                        pltpu.VMEM((nt, KEY_TILE, hq), F32),
                        pltpu.VMEM((KV_RANK, hq), F32),
                        pltpu.VMEM((SUBLANES, Q_BLOCK), I32)],
        compiler_params=_cparams(("parallel", "arbitrary")),
        name="dsa",
    )(qidx_t, qlat_t, widx, kidx.reshape(b, nt, KEY_TILE, LANES), ckv.reshape(b, nt, KEY_TILE, LANES),
      ckv_t.reshape(b, nt, KV_RANK, KEY_TILE), bias_tab, wuv_t)


FF_CHUNK = 256
TAIL_TILE = 512


def _mlstm_chunk(qk, sm, v_ref, o_ref, rows, gbias, ng_ref, consts, state):
    L = CHUNK
    causal, lower, upper, is_f, ones = consts
    c_st, m_st = state
    hs = range(M_HEADS)
    exact = dict(preferred_element_type=F32, precision=lax.Precision.HIGHEST)
    kt_all = qk[:, M_HEADS * M_QK:].T
    g = sm + gbias
    xg = jnp.where(is_f, jax.nn.log_sigmoid(g), g)
    rows_if = xg.T[SM_I:SM_I + 2 * M_HEADS, :]
    cum_r = jnp.dot(rows_if, upper, **exact)
    bc = jnp.dot(lower, xg, **exact)
    bc_q = [jnp.broadcast_to(bc[:, SM_F + h:SM_F + h + 1], (L, LANES)) for h in hs]
    b_r = [rows_if[h:h + 1, :] - cum_r[M_HEADS + h:M_HEADS + h + 1, :] for h in hs]
    b_last = [bc_q[h][L - 1:L, :] for h in hs]
    d_log = [jnp.where(causal, bc_q[h] + b_r[h], -jnp.inf) for h in hs]
    inter = [bc_q[h] + m_st[h] for h in hs]
    m_row = [jnp.maximum(inter[h], jnp.max(d_log[h], axis=-1, keepdims=True)) for h in hs]
    w_intra = [jnp.exp(d_log[h] - m_row[h]) for h in hs]
    w_inter = [jnp.exp(inter[h] - m_row[h]) for h in hs]
    floor = [jnp.exp(-m_row[h]) for h in hs]

    qb = [(qk[:, h * M_QK:(h + 1) * M_QK] * (M_QK ** -0.5)).astype(BF16) for h in hs]
    kt = [kt_all[h * M_QK:(h + 1) * M_QK, :] for h in hs]
    vaug = [jnp.concatenate([v_ref[rows, h * M_V:(h + 1) * M_V], ones], axis=1) for h in hs]
    sc = [_dot(qb[h], kt[h].astype(BF16)) for h in hs]
    cross = [_dot(qb[h], c_st[h].astype(BF16)) for h in hs]

    g_r = [b_last[h] + b_r[h] for h in hs]
    m_new = [jnp.maximum(b_last[h] + m_st[h], jnp.max(g_r[h], axis=-1, keepdims=True)) for h in hs]
    wk_r = [jnp.exp(g_r[h] - m_new[h]) for h in hs]
    decay = [jnp.exp(b_last[h] + m_st[h] - m_new[h]) for h in hs]
    upd = [_dot((kt[h] * wk_r[h]).astype(BF16), vaug[h]) for h in hs]

    sc = [(sc[h] * w_intra[h]).astype(BF16) for h in hs]
    num = [_dot(sc[h], vaug[h]) + jnp.concatenate([w_inter[h], w_inter[h]], axis=1) * cross[h] for h in hs]
    out = [num[h][:, :M_V] / jnp.maximum(jnp.abs(num[h][:, M_V:]), floor[h]) for h in hs]
    c_new = [jnp.concatenate([decay[h], decay[h]], axis=1) * c_st[h] + upd[h] for h in hs]
    y = [_group_norm(out[h], ng_ref[0:1, h * M_V:(h + 1) * M_V]) for h in hs]
    hm = [(y[h] * jax.nn.sigmoid(o_ref[rows, h * M_V:(h + 1) * M_V].astype(F32))).astype(BF16) for h in hs]
    return jnp.concatenate(hm, axis=1), (c_new, m_new)


def _ret_chunk(q_ref, k_ref, v_ref, g_ref, rows, cosf, sinf, intra_ref, qdec_ref, kdec_ref, cdec_ref,
               ng_ref, state):
    hs = range(R_HEADS)
    sl = [slice(h * R_QK, (h + 1) * R_QK) for h in hs]
    half = R_QK // 2

    def rot(x):
        return x * cosf + pltpu.roll(x, half, 1) * sinf

    qr = [rot(q_ref[rows, sl[h]]).astype(BF16) for h in hs]
    kt = [(rot(k_ref[rows, sl[h]]) * (R_QK ** -0.5)).T for h in hs]
    vb = [v_ref[rows, sl[h]] for h in hs]
    sc = [_dot(qr[h], kt[h].astype(BF16)) for h in hs]
    cross = [_dot(qr[h], state[h].astype(BF16)) for h in hs]
    upd = [_dot((kt[h] * kdec_ref[h:h + 1, :]).astype(BF16), vb[h]) for h in hs]
    sc = [(sc[h] * intra_ref[h]).astype(BF16) for h in hs]
    out = [_dot(sc[h], vb[h]) + qdec_ref[h] * cross[h] for h in hs]
    new = [cdec_ref[h:h + 1, :] * state[h] + upd[h] for h in hs]
    y = [_group_norm(out[h], ng_ref[0:1, sl[h]]) for h in hs]
    hr = []
    for h in hs:
        gt = g_ref[rows, sl[h]].astype(F32)
        hr.append((y[h] * (gt * jax.nn.sigmoid(gt))).astype(BF16))
    return jnp.concatenate(hr, axis=1), new


def _tail_kernel(x_ref, ha_ref, gm_ref, ga_ref, gr_ref,
                 qk_ref, sm_ref, mv_ref, mo_ref, rq_ref, rk_ref, rv_ref, rg_ref, cos_ref, sin_ref,
                 conv_ref, gb_ref, mng_ref, intra_ref, qdec_ref, kdec_ref, cdec_ref, rng_ref,
                 pm_ref, pa_ref, pr_ref, wo_ref, n2_ref, wg_ref, wu_ref, wd_ref, fg_ref,
                 out_ref, hm_scr, hr_scr, xbuf, cst, mst, rst, *, final, n_tiles, tiles_per_row):
    L = CHUNK
    TM = TAIL_TILE
    i = pl.program_id(0)
    scan_tile = jnp.minimum(i, n_tiles - 1)
    cur = lax.rem(i, 2)
    prev = lax.rem(i + 1, 2)

    @pl.when(i == 0)
    def _():
        hm_scr[...] = jnp.zeros_like(hm_scr)
        hr_scr[...] = jnp.zeros_like(hr_scr)

    @pl.when(lax.rem(scan_tile, tiles_per_row) == 0)
    def _():
        xbuf[0:SUBLANES, :] = jnp.zeros((SUBLANES, 2 * M_HEADS * M_QK), F32)
        cst[...] = jnp.zeros_like(cst)
        mst[...] = jnp.zeros_like(mst)
        rst[...] = jnp.zeros_like(rst)

    y = (jax.nn.sigmoid(gm_ref[...].astype(F32)) * _dot(hm_scr[prev], pm_ref[...])
         + jax.nn.sigmoid(ga_ref[...].astype(F32)) * _dot(ha_ref[...], pa_ref[...])
         + jax.nn.sigmoid(gr_ref[...].astype(F32)) * _dot(hr_scr[prev], pr_ref[...]))
    x1 = x_ref[...] + _dot(y.astype(BF16), wo_ref[...])
    h2 = _rms(x1, n2_ref[0:1, :]).astype(BF16)

    x_qk = qk_ref[...]
    xbuf[SUBLANES:SUBLANES + TM, :] = x_qk
    w = conv_ref[...]
    conv = w[CONV_W - 1:CONV_W, :] * x_qk
    for j in range(CONV_W - 1):
        off = SUBLANES - (CONV_W - 1) + j
        conv = conv + w[j:j + 1, :] * xbuf[off:off + TM, :]
    xbuf[0:SUBLANES, :] = x_qk[TM - SUBLANES:TM, :]
    qk_all = conv * jax.nn.sigmoid(conv)

    r_i = lax.broadcasted_iota(I32, (L, L), 0)
    c_i = lax.broadcasted_iota(I32, (L, L), 1)
    lane = lax.broadcasted_iota(I32, (L, LANES), 1)
    causal = r_i >= c_i
    consts = (causal, causal.astype(F32), (r_i <= c_i).astype(F32),
              (lane >= SM_F) & (lane < SM_F + M_HEADS), jnp.ones((L, M_V), BF16))
    gbias = gb_ref[0:1, :]
    m_state = ([cst[h] for h in range(M_HEADS)], [mst[h:h + 1, :] for h in range(M_HEADS)])
    r_state = [rst[h] for h in range(R_HEADS)]

    def scan_unit(u, m_state, r_state):
        c = u // 2
        rows = slice(c * L, (c + 1) * L)
        if u % 2 == 0:
            hm, m_state = _mlstm_chunk(qk_all[rows], sm_ref[rows, :], mv_ref, mo_ref, rows, gbias,
                                       mng_ref, consts, m_state)
            hm_scr[cur, rows, :] = hm
        else:
            hr, r_state = _ret_chunk(rq_ref, rk_ref, rv_ref, rg_ref, rows, cos_ref[rows, :],
                                     sin_ref[rows, :], intra_ref, qdec_ref, kdec_ref, cdec_ref,
                                     rng_ref, r_state)
            hr_scr[cur, rows, :] = hr
        return m_state, r_state

    n_units = 2 * (TM // L)
    n_ff = D_FF // FF_CHUNK
    assert n_units <= n_ff
    acc = x1
    for c in range(n_ff):
        cs = slice(c * FF_CHUNK, (c + 1) * FF_CHUNK)
        gate = _dot(h2, wg_ref[:, cs])
        up = _dot(h2, wu_ref[:, cs])
        act = (gate * jax.nn.sigmoid(gate) * up).astype(BF16)
        acc = acc + _dot(act, wd_ref[cs, :])
        if c < n_units:
            m_state, r_state = scan_unit(c, m_state, r_state)

    for h in range(M_HEADS):
        cst[h] = m_state[0][h]
    mst[0:M_HEADS, :] = jnp.concatenate(m_state[1], axis=0)
    for h in range(R_HEADS):
        rst[h] = r_state[h]
    if final:
        acc = _rms(acc, fg_ref[0:1, :])
    out_ref[...] = acc


def _tail(x2, ha, zf, zb, cosf, sinf, conv_w, gbias, mng, intra, qdec, kdec, cdec, rng,
          pm, pa, pr, wo, n2, wg, wu, wd, fg, s, final):
    t = x2.shape[0]
    tm = TAIL_TILE
    n_tiles = t // tm
    tiles_per_row = s // tm
    w512 = 2 * M_HEADS * M_QK
    merge_tile = lambda i: jnp.maximum(i - 1, 0)
    scan_tile = lambda i: jnp.minimum(i, n_tiles - 1)
    mspec = lambda w, cb=0: pl.BlockSpec((tm, w), lambda i: (merge_tile(i), cb))
    sspec = lambda w, off: pl.BlockSpec((tm, w), lambda i: (scan_tile(i), off // w))
    pspec = pl.BlockSpec((tm, R_QK), lambda i: (lax.rem(scan_tile(i), tiles_per_row), 0))
    return pl.pallas_call(
        functools.partial(_tail_kernel, final=final, n_tiles=n_tiles, tiles_per_row=tiles_per_row),
        grid=(n_tiles + 1,),
        in_specs=[mspec(D_MODEL), mspec(A_WIDTH),
                  mspec(D_MODEL, B_GM // D_MODEL), mspec(D_MODEL, B_GA // D_MODEL),
                  mspec(D_MODEL, B_GR // D_MODEL),
                  sspec(w512, F_MQK), sspec(LANES, F_SMALL), sspec(M_WIDTH, B_MV), sspec(M_WIDTH, B_MO),
                  sspec(R_WIDTH, F_RQ), sspec(R_WIDTH, F_RK), sspec(R_WIDTH, B_RV), sspec(R_WIDTH, B_RG),
                  pspec, pspec,
                  _resident((SUBLANES, w512)), _resident((SUBLANES, LANES)), _resident((SUBLANES, M_WIDTH)),
                  _resident((R_HEADS, CHUNK, CHUNK)), _resident((R_HEADS, CHUNK, R_V)),
                  _resident((SUBLANES, CHUNK)), _resident((SUBLANES, R_V)), _resident((SUBLANES, R_WIDTH)),
                  _resident((M_WIDTH, D_MODEL)), _resident((A_WIDTH, D_MODEL)),
                  _resident((R_WIDTH, D_MODEL)), _resident((D_MODEL, D_MODEL)),
                  _resident((SUBLANES, D_MODEL)),
                  _resident((D_MODEL, D_FF)), _resident((D_MODEL, D_FF)),
                  _resident((D_FF, D_MODEL)), _resident((SUBLANES, D_MODEL))],
        out_specs=mspec(D_MODEL),
        out_shape=jax.ShapeDtypeStruct((t, D_MODEL), F32),
        scratch_shapes=[pltpu.VMEM((2, tm, M_WIDTH), BF16),
                        pltpu.VMEM((2, tm, R_WIDTH), BF16),
                        pltpu.VMEM((SUBLANES + tm, w512), F32),
                        pltpu.VMEM((M_HEADS, M_QK, 2 * M_V), F32),
                        pltpu.VMEM((SUBLANES, LANES), F32),
                        pltpu.VMEM((R_HEADS, R_QK, R_V), F32)],
        compiler_params=_cparams(("arbitrary",), VMEM_LIMIT_TAIL),
        name="tail",
    )(x2, ha, zb, zb, zb, zf, zf, zb, zb, zf, zf, zb, zb, cosf, sinf,
      conv_w, gbias, mng, intra, qdec, kdec, cdec, rng, pm, pa, pr, wo, n2, wg, wu, wd, fg)


def _pad_rows(v, rows=SUBLANES):
    v = jnp.atleast_2d(v).astype(F32)
    return jnp.pad(v, ((0, rows - v.shape[0]), (0, 0)))


def _t5_bucket(dist):
    max_exact = N_BUCKETS // 2
    d_f = jnp.maximum(dist, 1).astype(F32)
    large = max_exact + (jnp.log(d_f / max_exact) / math.log(MAX_DISTANCE / max_exact)
                         * (N_BUCKETS - max_exact)).astype(I32)
    large = jnp.minimum(large, N_BUCKETS - 1)
    return jnp.where(dist < max_exact, dist, large)


def _bias_table(rel_bias):
    d = jnp.arange(N_BIAS_TABLES, dtype=I32)[:, None, None]
    j = jnp.arange(KEY_TILE, dtype=I32)[None, :, None]
    i = jnp.arange(Q_BLOCK, dtype=I32)[None, None, :]
    bucket = _t5_bucket(jnp.maximum(i + d * Q_BLOCK - j, 0))
    onehot = (bucket[None] == jnp.arange(N_BUCKETS, dtype=I32)[:, None, None, None]).astype(F32)
    return jnp.einsum("nh,ndkq->dhkq", rel_bias.astype(F32), onehot, precision=lax.Precision.HIGHEST)


def _rotary_tables(s):
    half = R_QK // 2
    freqs = ROPE_BASE ** (-jnp.linspace(0.0, 1.0, half, dtype=F32))
    ang = jnp.arange(s, dtype=F32)[:, None] * freqs[None, :]
    cos, sin = jnp.cos(ang), jnp.sin(ang)
    return jnp.concatenate([cos, cos], axis=1), jnp.concatenate([-sin, sin], axis=1)


def _retention_tables():
    h = R_HEADS
    log_gamma = jnp.log1p(-jnp.exp2(-5.0 - jnp.arange(h, dtype=F32)))
    pos = jnp.arange(CHUNK, dtype=F32)
    diff = pos[:, None] - pos[None, :]
    intra = jnp.where(diff >= 0, jnp.exp(jnp.maximum(diff, 0.0)[None] * log_gamma[:, None, None]), 0.0)
    q_decay = jnp.exp((pos[None, :] + 1.0) * log_gamma[:, None])
    k_decay = jnp.exp((CHUNK - 1.0 - pos[None, :]) * log_gamma[:, None])
    chunk_decay = jnp.exp(CHUNK * log_gamma)
    qdec = jnp.broadcast_to(q_decay[:, :, None], (h, CHUNK, R_V))
    kdec = _pad_rows(k_decay)
    cdec = _pad_rows(jnp.broadcast_to(chunk_decay[:, None], (h, R_V)))
    return intra, qdec, kdec, cdec


def _layout_w_in(w_in):
    parts = jnp.split(w_in, [int(c) for c in np.cumsum(IN_SPLITS)[:-1]], axis=-1)
    (m_q, m_k, m_v, m_i, m_f, m_o, a_cq, a_kidx, a_widx, a_ckv,
     r_q, r_k, r_v, r_g, g_m, g_a, g_r) = parts
    small = jnp.concatenate([a_kidx, m_i, m_f, a_widx], axis=-1)
    small = jnp.pad(small, ((0, 0), (0, LANES - small.shape[-1])))
    w = jnp.concatenate([m_q, m_k, a_cq, a_ckv, small, r_q, r_k,
                         m_v, m_o, r_v, r_g, g_m, g_a, g_r], axis=-1)
    assert w.shape[-1] == ZF_WIDTH + ZB_WIDTH
    return w.astype(BF16)


def _row_tile(t, want):
    while t % want:
        want //= 2
    return want


def kernel(x, rel_bias, final_norm_g, norm1_g, w_in, m_conv, m_ibias, m_fbias, m_norm_g, a_qnorm_g, a_wuq, a_wuq_idx, a_kidx_g, a_kvnorm_g, a_wuk, a_wuv, r_norm_g, p_m, p_a, p_r, w_out, norm2_g, w_gate, w_up, w_down):
    b, s, d = x.shape
    depth = w_in.shape[0]
    assert d == D_MODEL and s % KEY_TILE == 0 and s % TAIL_TILE == 0
    assert SM_F == SM_I + M_HEADS and M_V == LANES and R_V == LANES and CHUNK == LANES
    t = b * s
    tm_in = _row_tile(t, 2048)
    tm = _row_tile(t, 512)

    bias_tab = _bias_table(rel_bias)
    cosf, sinf = _rotary_tables(s)
    intra, qdec, kdec, cdec = _retention_tables()
    fg = _pad_rows(final_norm_g)

    x2 = x.reshape(t, d)
    for l in range(depth):
        zf, zb = _inproj(x2, norm1_g[l][None, :], _layout_w_in(w_in[l]), tm_in, 1024)

        gbias = jnp.zeros((LANES,), F32)
        gbias = gbias.at[SM_I:SM_I + M_HEADS].set(m_ibias[l]).at[SM_F:SM_F + M_HEADS].set(m_fbias[l])

        wuqi_t = a_wuq_idx[l].reshape(Q_RANK, IDX_HEADS, IDX_DIM).transpose(1, 2, 0)
        wuqi_t = jnp.pad(wuqi_t, ((0, 0), (0, LANES - IDX_DIM), (0, 0))).astype(BF16)
        kig = jnp.pad(a_kidx_g[l], (0, LANES - IDX_DIM))
        qlat_t, qidx_t, widx, kidx, ckv, ckv_t = _dsa_prep(
            zf, _pad_rows(a_qnorm_g[l]), _pad_rows(a_kvnorm_g[l]), _pad_rows(kig),
            a_wuq[l].T.astype(BF16), wuqi_t, jnp.swapaxes(a_wuk[l], 1, 2).astype(BF16), tm)
        ha = _dsa(qidx_t, qlat_t, widx, kidx, ckv, ckv_t, bias_tab,
                  jnp.swapaxes(a_wuv[l], 1, 2).astype(BF16), b, s)

        x2 = _tail(x2, ha, zf, zb, cosf, sinf, _pad_rows(m_conv[l]), _pad_rows(gbias),
                   _pad_rows(m_norm_g[l]), intra, qdec, kdec, cdec, _pad_rows(r_norm_g[l]),
                   p_m[l].astype(BF16), p_a[l].astype(BF16), p_r[l].astype(BF16),
                   w_out[l].astype(BF16), _pad_rows(norm2_g[l]), w_gate[l].astype(BF16),
                   w_up[l].astype(BF16), w_down[l].astype(BF16), fg, s, final=(l == depth - 1))
    return x2.reshape(b, s, d)
```

```python
import functools
import math

import numpy as np
import jax
import jax.numpy as jnp
from jax import lax
from jax.experimental import pallas as pl
from jax.experimental.pallas import tpu as pltpu

F32 = jnp.float32
BF16 = jnp.bfloat16
I32 = jnp.int32

D_MODEL = 1024
M_HEADS, M_QK, M_V = 4, 64, 128
M_WIDTH = M_HEADS * M_V
CONV_W = 4
A_HEADS, A_HEAD_DIM = 8, 64
A_WIDTH = A_HEADS * A_HEAD_DIM
Q_RANK, KV_RANK = 256, 128
IDX_HEADS, IDX_DIM = 8, 64
TOPK_MAX = 256
R_HEADS, R_QK, R_V = 4, 128, 128
R_WIDTH = R_HEADS * R_V
CHUNK = 128
Q_BLOCK = 256
N_BUCKETS = 32
MAX_DISTANCE = 128
ROPE_BASE = 10000.0
EPS = 1e-6
D_FF = -(((-8 * D_MODEL) // 3) // 256) * 256
IN_SPLITS = (M_HEADS * M_QK, M_HEADS * M_QK, M_WIDTH, M_HEADS, M_HEADS, M_WIDTH,
             Q_RANK, IDX_DIM, IDX_HEADS, KV_RANK,
             R_HEADS * R_QK, R_HEADS * R_QK, R_WIDTH, R_WIDTH,
             D_MODEL, D_MODEL, D_MODEL)

LANES = 128
SUBLANES = 8
VMEM_LIMIT = 56 * 1024 * 1024
VMEM_LIMIT_TAIL = 62 * 1024 * 1024

F_MQK = 0
F_ACQ = 512
F_ACKV = 768
F_SMALL = 896
F_RQ, F_RK = 1024, 1536
ZF_WIDTH = 2048
B_MV, B_MO, B_RV, B_RG = 0, 512, 1024, 1536
B_GM, B_GA, B_GR = 2048, 3072, 4096
ZB_WIDTH = 5120
SM_KIDX, SM_I, SM_F, SM_W = 0, 64, 68, 72

KEY_TILE = 256
N_BIAS_TABLES = -(-(MAX_DISTANCE + KEY_TILE - 1) // Q_BLOCK) + 1
INT_MIN = -2 ** 31


def _cparams(sem, vmem_limit=VMEM_LIMIT):
    return pltpu.CompilerParams(dimension_semantics=sem, vmem_limit_bytes=vmem_limit)


def _resident(shape):
    nd = len(shape)
    return pl.BlockSpec(shape, lambda *_: (0,) * nd, pipeline_mode=pl.Buffered(1))


def _rms(x, g):
    return x * lax.rsqrt(jnp.mean(x * x, axis=-1, keepdims=True) + EPS) * g


def _group_norm(x, g):
    mu = jnp.mean(x, axis=-1, keepdims=True)
    xc = x - mu
    return xc * lax.rsqrt(jnp.mean(xc * xc, axis=-1, keepdims=True) + EPS) * g


def _dot(a, b):
    return jnp.dot(a, b, preferred_element_type=F32)


def _inproj_kernel(x_ref, g_ref, w_ref, zf_ref, zb_ref, h_scr, *, nf):
    j = pl.program_id(1)

    @pl.when(j == 0)
    def _():
        h_scr[...] = _rms(x_ref[...], g_ref[...]).astype(BF16)

    @pl.when(j < nf)
    def _():
        zf_ref[...] = _dot(h_scr[...], w_ref[...])

    @pl.when(j >= nf)
    def _():
        zb_ref[...] = _dot(h_scr[...], w_ref[...]).astype(BF16)


def _inproj(x2, g, w, tm, tn):
    t = x2.shape[0]
    nf = ZF_WIDTH // tn
    return pl.pallas_call(
        functools.partial(_inproj_kernel, nf=nf),
        grid=(t // tm, (ZF_WIDTH + ZB_WIDTH) // tn),
        in_specs=[pl.BlockSpec((tm, D_MODEL), lambda i, j: (i, 0)),
                  pl.BlockSpec((1, D_MODEL), lambda i, j: (0, 0)),
                  pl.BlockSpec((D_MODEL, tn), lambda i, j: (0, j))],
        out_specs=[pl.BlockSpec((tm, tn), lambda i, j: (i, jnp.minimum(j, nf - 1))),
                   pl.BlockSpec((tm, tn), lambda i, j: (i, jnp.maximum(j - nf, 0)))],
        out_shape=[jax.ShapeDtypeStruct((t, ZF_WIDTH), F32),
                   jax.ShapeDtypeStruct((t, ZB_WIDTH), BF16)],
        scratch_shapes=[pltpu.VMEM((tm, D_MODEL), BF16)],
        compiler_params=_cparams(("parallel", "arbitrary")),
        name="inproj",
    )(x2, g, w)


def _dsa_prep_kernel(cq_ref, ckv_ref, sm_ref, qn_ref, kvn_ref, kig_ref, wuq_t_ref, wuqi_t_ref,
                     wuk_t_ref, qlat_ref, qidx_ref, widx_ref, kidx_ref, ckvn_ref, ckvt_ref):
    tm = cq_ref.shape[0]
    c_qt = _rms(cq_ref[...], qn_ref[0:1, :]).T.astype(BF16)
    q_t = _dot(wuq_t_ref[...], c_qt).astype(BF16)
    for h in range(A_HEADS):
        ql_t = (_dot(wuk_t_ref[h], q_t[h * A_HEAD_DIM:(h + 1) * A_HEAD_DIM, :])
                * (A_HEAD_DIM ** -0.5)).astype(BF16)
        qi_t = _dot(wuqi_t_ref[h], c_qt).astype(BF16)
        for j in range(tm // Q_BLOCK):
            cols = slice(j * Q_BLOCK, (j + 1) * Q_BLOCK)
            qlat_ref[j, :, h * Q_BLOCK:(h + 1) * Q_BLOCK] = ql_t[:, cols]
            qidx_ref[j, :, h * Q_BLOCK:(h + 1) * Q_BLOCK] = qi_t[:, cols]
    sm = sm_ref[...]
    w_t = sm.T[SM_W:SM_W + IDX_HEADS, :] * (IDX_HEADS ** -0.5 * IDX_DIM ** -0.5)
    for j in range(tm // Q_BLOCK):
        widx_ref[j] = w_t[:, j * Q_BLOCK:(j + 1) * Q_BLOCK]
    lane = lax.broadcasted_iota(I32, sm.shape, 1)
    inside = lane < IDX_DIM
    mu = jnp.sum(jnp.where(inside, sm, 0.0), axis=-1, keepdims=True) * (1.0 / IDX_DIM)
    xc = jnp.where(inside, sm - mu, 0.0)
    var = jnp.sum(xc * xc, axis=-1, keepdims=True) * (1.0 / IDX_DIM)
    kidx_ref[...] = (xc * lax.rsqrt(var + EPS) * kig_ref[0:1, :]).astype(BF16)
    ckvn = _rms(ckv_ref[...], kvn_ref[0:1, :])
    ckvn_ref[...] = ckvn.astype(BF16)
    ckvn_t = ckvn.T.astype(BF16)
    for j in range(tm // KEY_TILE):
        ckvt_ref[j] = ckvn_t[:, j * KEY_TILE:(j + 1) * KEY_TILE]


def _dsa_prep(z, qn, kvn, kig, wuq_t, wuqi_t, wuk_t, tm):
    t = z.shape[0]
    hq = A_HEADS * Q_BLOCK
    nqb = tm // Q_BLOCK
    qspec = pl.BlockSpec((nqb, LANES, hq), lambda i: (i, 0, 0))
    return pl.pallas_call(
        _dsa_prep_kernel,
        grid=(t // tm,),
        in_specs=[pl.BlockSpec((tm, Q_RANK), lambda i: (i, F_ACQ // Q_RANK)),
                  pl.BlockSpec((tm, KV_RANK), lambda i: (i, F_ACKV // KV_RANK)),
                  pl.BlockSpec((tm, LANES), lambda i: (i, F_SMALL // LANES)),
                  _resident((SUBLANES, Q_RANK)),
                  _resident((SUBLANES, KV_RANK)),
                  _resident((SUBLANES, LANES)),
                  _resident((A_WIDTH, Q_RANK)),
                  _resident((IDX_HEADS, LANES, Q_RANK)),
                  _resident((A_HEADS, KV_RANK, A_HEAD_DIM))],
        out_specs=[qspec, qspec,
                   pl.BlockSpec((nqb, IDX_HEADS, Q_BLOCK), lambda i: (i, 0, 0)),
                   pl.BlockSpec((tm, LANES), lambda i: (i, 0)),
                   pl.BlockSpec((tm, KV_RANK), lambda i: (i, 0)),
                   pl.BlockSpec((tm // KEY_TILE, KV_RANK, KEY_TILE), lambda i: (i, 0, 0))],
        out_shape=[jax.ShapeDtypeStruct((t // Q_BLOCK, KV_RANK, hq), BF16),
                   jax.ShapeDtypeStruct((t // Q_BLOCK, LANES, hq), BF16),
                   jax.ShapeDtypeStruct((t // Q_BLOCK, IDX_HEADS, Q_BLOCK), F32),
                   jax.ShapeDtypeStruct((t, LANES), BF16),
                   jax.ShapeDtypeStruct((t, KV_RANK), BF16),
                   jax.ShapeDtypeStruct((t // KEY_TILE, KV_RANK, KEY_TILE), BF16)],
        compiler_params=_cparams(("parallel",)),
        name="dsa_prep",
    )(z, z, z, qn, kvn, kig, wuq_t, wuqi_t, wuk_t)


def _fold(x, op, rows=SUBLANES):
    parts = [x[i:i + rows] for i in range(0, x.shape[0], rows)]
    while len(parts) > 1:
        parts = [op(parts[i], parts[i + 1]) for i in range(0, len(parts), 2)]
    return parts[0]


def _tile_loop(n, body, init, quads=False):
    def pair(j, carry):
        return body(j + 1, body(j, carry))

    def quad(j, carry):
        return pair(4 * j + 2, pair(4 * j, carry))

    start = 0
    carry = init
    if quads:
        carry = lax.fori_loop(0, n // 4, quad, carry)
        start = 4 * (n // 4)
    npair = (n - start) // 2
    carry = lax.fori_loop(0, npair, lambda j, c: pair(start + 2 * j, c), carry)
    return lax.fori_loop(start + 2 * npair, n, body, carry)


def _dsa_kernel(qidx_ref, qlat_ref, widx_ref, kidx_ref, ckv_ref, ckvt_ref, bias_ref, wuv_t_ref,
                out_ref, keys_scr, lg_scr, acc_scr, tie_scr, *, top_k, n_keys):
    QB = Q_BLOCK
    KT = KEY_TILE
    qblk = pl.program_id(1)
    nk = (qblk * QB + QB + KT - 1) // KT
    kf = float(top_k)
    hcols = [slice(h * QB, (h + 1) * QB) for h in range(A_HEADS)]
    hrows = [slice(h * SUBLANES, (h + 1) * SUBLANES) for h in range(A_HEADS)]

    q_pos = qblk * QB + lax.broadcasted_iota(I32, (KT, QB), 1)
    key_off = lax.broadcasted_iota(I32, (KT, QB), 0)

    qi_t = qidx_ref[0]
    w_rows = widx_ref[0]

    def score_body(kt, carry):
        s_t = _dot(kidx_ref[0, kt], qi_t)
        tot = jnp.zeros((KT, QB), F32)
        for g in range(IDX_HEADS):
            tot = tot + w_rows[g:g + 1, :] * jnp.maximum(s_t[:, hcols[g]], 0.0)
        key_pos = kt * KT + key_off
        tot = jnp.where(key_pos <= q_pos, tot + 0.0, -jnp.inf)
        bits = pltpu.bitcast(tot, I32)
        keys_scr[kt] = bits ^ ((bits >> 31) & 0x7FFFFFFF)
        return carry

    _tile_loop(nk, score_body, 0, quads=True)

    def count(pred):
        def body(kt, acc):
            return acc + _fold(jnp.where(pred(keys_scr[kt], kt), 1.0, 0.0), jnp.add)
        acc = _tile_loop(nk, body, jnp.zeros((SUBLANES, QB), F32))
        return jnp.sum(acc, axis=0, keepdims=True)

    def count_ge(cand):
        return count(lambda keys, kt: keys >= cand)

    n_bits = jnp.where((qblk + 1) * QB <= top_k, 0, 31)
    thr = jnp.where((count_ge(jnp.zeros((1, QB), I32)) >= kf) & (n_bits > 0), 0, INT_MIN).astype(I32)

    def bit_body(i, thr):
        cand = thr | (jnp.int32(1) << (30 - i))
        return jnp.where(count_ge(cand) >= kf, cand, thr)

    thr = lax.fori_loop(0, n_bits, bit_body, thr)

    tie_scr[...] = jnp.full((SUBLANES, QB), n_keys, I32)
    over = jnp.max(jnp.where(count_ge(thr) > kf, 1.0, 0.0))

    @pl.when(over > 0.0)
    def _():
        need = kf - count(lambda keys, kt: keys > thr)

        def idx_body(i, p):
            cand = p | (jnp.int32(1) << (n_keys.bit_length() - 1 - i))
            cnt = count(lambda keys, kt: (keys == thr) & (kt * KT + key_off < cand))
            return jnp.where(cnt < need, cand, p)

        p = lax.fori_loop(0, n_keys.bit_length(), idx_body, jnp.zeros((1, QB), I32))
        tie_scr[...] = jnp.broadcast_to(p, (SUBLANES, QB))

    tie_hi = tie_scr[0:1, :]

    ql_t = qlat_ref[0]

    def pass_a(kt, mx):
        lg_t = _dot(ckv_ref[0, kt], ql_t)
        keys = keys_scr[kt]
        key_pos = kt * KT + key_off
        sel = (keys > thr) | ((keys == thr) & (key_pos <= tie_hi))
        sel = sel & (key_pos <= q_pos)
        didx = jnp.minimum((qblk * QB - kt * KT) // QB, N_BIAS_TABLES - 1)
        new = []
        for h in range(A_HEADS):
            lh = jnp.where(sel, lg_t[:, hcols[h]] + bias_ref[didx, h], -jnp.inf)
            lg_scr[kt, :, hcols[h]] = lh
            new.append(jnp.maximum(mx[hrows[h]], _fold(lh, jnp.maximum)))
        return jnp.concatenate(new, axis=0)

    mx = _tile_loop(nk, pass_a, jnp.full((A_HEADS * SUBLANES, QB), -jnp.inf, F32), quads=True)
    m_rows = [jnp.max(mx[hrows[h]], axis=0, keepdims=True) for h in range(A_HEADS)]

    acc_scr[...] = jnp.zeros_like(acc_scr)

    def pass_b(kt, l):
        lg = lg_scr[kt]
        ps, new = [], []
        for h in range(A_HEADS):
            p = jnp.exp(lg[:, hcols[h]] - m_rows[h])
            new.append(l[hrows[h]] + _fold(p, jnp.add))
            ps.append(p.astype(BF16))
        acc_scr[...] = acc_scr[...] + _dot(ckvt_ref[0, kt], jnp.concatenate(ps, axis=1))
        return jnp.concatenate(new, axis=0)

    l = _tile_loop(nk, pass_b, jnp.zeros((A_HEADS * SUBLANES, QB), F32), quads=True)

    outs = []
    for h in range(A_HEADS):
        o_t = (acc_scr[:, hcols[h]] / jnp.sum(l[hrows[h]], axis=0, keepdims=True)).astype(BF16)
        outs.append(_dot(wuv_t_ref[h], o_t))
    out_ref[...] = jnp.concatenate(outs, axis=0).T.astype(BF16)


def _dsa(qidx_t, qlat_t, widx, kidx, ckv, ckv_t, bias_tab, wuv_t, b, s):
    nq = s // Q_BLOCK
    nt = s // KEY_TILE
    top_k = min(TOPK_MAX, s // 4)
    hq = A_HEADS * Q_BLOCK
    row = lambda bi, qi: bi * nq + qi
    qspec = pl.BlockSpec((1, LANES, hq), lambda bi, qi: (row(bi, qi), 0, 0))
    kspec = pl.BlockSpec((1, nt, KEY_TILE, LANES), lambda bi, qi: (bi, 0, 0, 0))
    return pl.pallas_call(
        functools.partial(_dsa_kernel, top_k=top_k, n_keys=s),
        grid=(b, nq),
        in_specs=[qspec, qspec,
                  pl.BlockSpec((1, IDX_HEADS, Q_BLOCK), lambda bi, qi: (row(bi, qi), 0, 0)),
                  kspec, kspec,
                  pl.BlockSpec((1, nt, KV_RANK, KEY_TILE), lambda bi, qi: (bi, 0, 0, 0)),
                  _resident((N_BIAS_TABLES, A_HEADS, KEY_TILE, Q_BLOCK)),
                  _resident((A_HEADS, A_HEAD_DIM, KV_RANK))],
        out_specs=pl.BlockSpec((Q_BLOCK, A_WIDTH), lambda bi, qi: (row(bi, qi), 0)),
        out_shape=jax.ShapeDtypeStruct((b * s, A_WIDTH), BF16),
        scratch_shapes=[pltpu.VMEM((nt, KEY_TILE, Q_BLOCK), I32),
                        pltpu.VMEM((nt, KEY_TILE, hq), F32),
                        pltpu.VMEM((KV_RANK, hq), F32),
                        pltpu.VMEM((SUBLANES, Q_BLOCK), I32)],
        compiler_params=_cparams(("parallel", "arbitrary")),
        name="dsa",
    )(qidx_t, qlat_t, widx, kidx.reshape(b, nt, KEY_TILE, LANES), ckv.reshape(b, nt, KEY_TILE, LANES),
      ckv_t.reshape(b, nt, KV_RANK, KEY_TILE), bias_tab, wuv_t)


FF_CHUNK = 256
TAIL_TILE = 512


def _mlstm_chunk(qk, sm, v_ref, o_ref, rows, gbias, ng_ref, consts, state):
    L = CHUNK
    causal, lower, upper, is_f, ones = consts
    c_st, m_st = state
    hs = range(M_HEADS)
    exact = dict(preferred_element_type=F32, precision=lax.Precision.HIGHEST)
    kt_all = qk[:, M_HEADS * M_QK:].T
    g = sm + gbias
    xg = jnp.where(is_f, jax.nn.log_sigmoid(g), g)
    rows_if = xg.T[SM_I:SM_I + 2 * M_HEADS, :]
    cum_r = jnp.dot(rows_if, upper, **exact)
    bc = jnp.dot(lower, xg, **exact)
    bc_q = [jnp.broadcast_to(bc[:, SM_F + h:SM_F + h + 1], (L, LANES)) for h in hs]
    b_r = [rows_if[h:h + 1, :] - cum_r[M_HEADS + h:M_HEADS + h + 1, :] for h in hs]
    b_last = [bc_q[h][L - 1:L, :] for h in hs]
    d_log = [jnp.where(causal, bc_q[h] + b_r[h], -jnp.inf) for h in hs]
    inter = [bc_q[h] + m_st[h] for h in hs]
    m_row = [jnp.maximum(inter[h], jnp.max(d_log[h], axis=-1, keepdims=True)) for h in hs]
    w_intra = [jnp.exp(d_log[h] - m_row[h]) for h in hs]
    w_inter = [jnp.exp(inter[h] - m_row[h]) for h in hs]
    floor = [jnp.exp(-m_row[h]) for h in hs]

    qb = [(qk[:, h * M_QK:(h + 1) * M_QK] * (M_QK ** -0.5)).astype(BF16) for h in hs]
    kt = [kt_all[h * M_QK:(h + 1) * M_QK, :] for h in hs]
    vaug = [jnp.concatenate([v_ref[rows, h * M_V:(h + 1) * M_V], ones], axis=1) for h in hs]
    sc = [_dot(qb[h], kt[h].astype(BF16)) for h in hs]
    cross = [_dot(qb[h], c_st[h].astype(BF16)) for h in hs]

    g_r = [b_last[h] + b_r[h] for h in hs]
    m_new = [jnp.maximum(b_last[h] + m_st[h], jnp.max(g_r[h], axis=-1, keepdims=True)) for h in hs]
    wk_r = [jnp.exp(g_r[h] - m_new[h]) for h in hs]
    decay = [jnp.exp(b_last[h] + m_st[h] - m_new[h]) for h in hs]
    upd = [_dot((kt[h] * wk_r[h]).astype(BF16), vaug[h]) for h in hs]

    sc = [(sc[h] * w_intra[h]).astype(BF16) for h in hs]
    num = [_dot(sc[h], vaug[h]) + jnp.concatenate([w_inter[h], w_inter[h]], axis=1) * cross[h] for h in hs]
    out = [num[h][:, :M_V] / jnp.maximum(jnp.abs(num[h][:, M_V:]), floor[h]) for h in hs]
    c_new = [jnp.concatenate([decay[h], decay[h]], axis=1) * c_st[h] + upd[h] for h in hs]
    y = [_group_norm(out[h], ng_ref[0:1, h * M_V:(h + 1) * M_V]) for h in hs]
    hm = [(y[h] * jax.nn.sigmoid(o_ref[rows, h * M_V:(h + 1) * M_V].astype(F32))).astype(BF16) for h in hs]
    return jnp.concatenate(hm, axis=1), (c_new, m_new)


def _ret_chunk(q_ref, k_ref, v_ref, g_ref, rows, cosf, sinf, intra_ref, qdec_ref, kdec_ref, cdec_ref,
               ng_ref, state):
    hs = range(R_HEADS)
    sl = [slice(h * R_QK, (h + 1) * R_QK) for h in hs]
    half = R_QK // 2

    def rot(x):
        return x * cosf + pltpu.roll(x, half, 1) * sinf

    qr = [rot(q_ref[rows, sl[h]]).astype(BF16) for h in hs]
    kt = [(rot(k_ref[rows, sl[h]]) * (R_QK ** -0.5)).T for h in hs]
    vb = [v_ref[rows, sl[h]] for h in hs]
    sc = [_dot(qr[h], kt[h].astype(BF16)) for h in hs]
    cross = [_dot(qr[h], state[h].astype(BF16)) for h in hs]
    upd = [_dot((kt[h] * kdec_ref[h:h + 1, :]).astype(BF16), vb[h]) for h in hs]
    sc = [(sc[h] * intra_ref[h]).astype(BF16) for h in hs]
    out = [_dot(sc[h], vb[h]) + qdec_ref[h] * cross[h] for h in hs]
    new = [cdec_ref[h:h + 1, :] * state[h] + upd[h] for h in hs]
    y = [_group_norm(out[h], ng_ref[0:1, sl[h]]) for h in hs]
    hr = []
    for h in hs:
        gt = g_ref[rows, sl[h]].astype(F32)
        hr.append((y[h] * (gt * jax.nn.sigmoid(gt))).astype(BF16))
    return jnp.concatenate(hr, axis=1), new


def _tail_kernel(x_ref, ha_ref, gm_ref, ga_ref, gr_ref,
                 qk_ref, sm_ref, mv_ref, mo_ref, rq_ref, rk_ref, rv_ref, rg_ref, cos_ref, sin_ref,
                 conv_ref, gb_ref, mng_ref, intra_ref, qdec_ref, kdec_ref, cdec_ref, rng_ref,
                 pm_ref, pa_ref, pr_ref, wo_ref, n2_ref, wg_ref, wu_ref, wd_ref, fg_ref,
                 out_ref, hm_scr, hr_scr, xbuf, cst, mst, rst, *, final, n_tiles, tiles_per_row):
    L = CHUNK
    TM = TAIL_TILE
    i = pl.program_id(0)
    scan_tile = jnp.minimum(i, n_tiles - 1)
    cur = lax.rem(i, 2)
    prev = lax.rem(i + 1, 2)

    @pl.when(i == 0)
    def _():
        hm_scr[...] = jnp.zeros_like(hm_scr)
        hr_scr[...] = jnp.zeros_like(hr_scr)

    @pl.when(lax.rem(scan_tile, tiles_per_row) == 0)
    def _():
        xbuf[0:SUBLANES, :] = jnp.zeros((SUBLANES, 2 * M_HEADS * M_QK), F32)
        cst[...] = jnp.zeros_like(cst)
        mst[...] = jnp.zeros_like(mst)
        rst[...] = jnp.zeros_like(rst)

    y = (jax.nn.sigmoid(gm_ref[...].astype(F32)) * _dot(hm_scr[prev], pm_ref[...])
         + jax.nn.sigmoid(ga_ref[...].astype(F32)) * _dot(ha_ref[...], pa_ref[...])
         + jax.nn.sigmoid(gr_ref[...].astype(F32)) * _dot(hr_scr[prev], pr_ref[...]))
    x1 = x_ref[...] + _dot(y.astype(BF16), wo_ref[...])
    h2 = _rms(x1, n2_ref[0:1, :]).astype(BF16)

    x_qk = qk_ref[...]
    xbuf[SUBLANES:SUBLANES + TM, :] = x_qk
    w = conv_ref[...]
    conv = w[CONV_W - 1:CONV_W, :] * x_qk
    for j in range(CONV_W - 1):
        off = SUBLANES - (CONV_W - 1) + j
        conv = conv + w[j:j + 1, :] * xbuf[off:off + TM, :]
    xbuf[0:SUBLANES, :] = x_qk[TM - SUBLANES:TM, :]
    qk_all = conv * jax.nn.sigmoid(conv)

    r_i = lax.broadcasted_iota(I32, (L, L), 0)
    c_i = lax.broadcasted_iota(I32, (L, L), 1)
    lane = lax.broadcasted_iota(I32, (L, LANES), 1)
    causal = r_i >= c_i
    consts = (causal, causal.astype(F32), (r_i <= c_i).astype(F32),
              (lane >= SM_F) & (lane < SM_F + M_HEADS), jnp.ones((L, M_V), BF16))
    gbias = gb_ref[0:1, :]
    m_state = ([cst[h] for h in range(M_HEADS)], [mst[h:h + 1, :] for h in range(M_HEADS)])
    r_state = [rst[h] for h in range(R_HEADS)]

    def scan_unit(u, m_state, r_state):
        c = u // 2
        rows = slice(c * L, (c + 1) * L)
        if u % 2 == 0:
            hm, m_state = _mlstm_chunk(qk_all[rows], sm_ref[rows, :], mv_ref, mo_ref, rows, gbias,
                                       mng_ref, consts, m_state)
            hm_scr[cur, rows, :] = hm
        else:
            hr, r_state = _ret_chunk(rq_ref, rk_ref, rv_ref, rg_ref, rows, cos_ref[rows, :],
                                     sin_ref[rows, :], intra_ref, qdec_ref, kdec_ref, cdec_ref,
                                     rng_ref, r_state)
            hr_scr[cur, rows, :] = hr
        return m_state, r_state

    n_units = 2 * (TM // L)
    n_ff = D_FF // FF_CHUNK
    assert n_units <= n_ff
    acc = x1
    for c in range(n_ff):
        cs = slice(c * FF_CHUNK, (c + 1) * FF_CHUNK)
        gate = _dot(h2, wg_ref[:, cs])
        up = _dot(h2, wu_ref[:, cs])
        act = (gate * jax.nn.sigmoid(gate) * up).astype(BF16)
        acc = acc + _dot(act, wd_ref[cs, :])
        if c < n_units:
            m_state, r_state = scan_unit(c, m_state, r_state)

    for h in range(M_HEADS):
        cst[h] = m_state[0][h]
    mst[0:M_HEADS, :] = jnp.concatenate(m_state[1], axis=0)
    for h in range(R_HEADS):
        rst[h] = r_state[h]
    if final:
        acc = _rms(acc, fg_ref[0:1, :])
    out_ref[...] = acc


def _tail(x2, ha, zf, zb, cosf, sinf, conv_w, gbias, mng, intra, qdec, kdec, cdec, rng,
          pm, pa, pr, wo, n2, wg, wu, wd, fg, s, final):
    t = x2.shape[0]
    tm = TAIL_TILE
    n_tiles = t // tm
    tiles_per_row = s // tm
    w512 = 2 * M_HEADS * M_QK
    merge_tile = lambda i: jnp.maximum(i - 1, 0)
    scan_tile = lambda i: jnp.minimum(i, n_tiles - 1)
    mspec = lambda w, cb=0: pl.BlockSpec((tm, w), lambda i: (merge_tile(i), cb))
    sspec = lambda w, off: pl.BlockSpec((tm, w), lambda i: (scan_tile(i), off // w))
    pspec = pl.BlockSpec((tm, R_QK), lambda i: (lax.rem(scan_tile(i), tiles_per_row), 0))
    return pl.pallas_call(
        functools.partial(_tail_kernel, final=final, n_tiles=n_tiles, tiles_per_row=tiles_per_row),
        grid=(n_tiles + 1,),
        in_specs=[mspec(D_MODEL), mspec(A_WIDTH),
                  mspec(D_MODEL, B_GM // D_MODEL), mspec(D_MODEL, B_GA // D_MODEL),
                  mspec(D_MODEL, B_GR // D_MODEL),
                  sspec(w512, F_MQK), sspec(LANES, F_SMALL), sspec(M_WIDTH, B_MV), sspec(M_WIDTH, B_MO),
                  sspec(R_WIDTH, F_RQ), sspec(R_WIDTH, F_RK), sspec(R_WIDTH, B_RV), sspec(R_WIDTH, B_RG),
                  pspec, pspec,
                  _resident((SUBLANES, w512)), _resident((SUBLANES, LANES)), _resident((SUBLANES, M_WIDTH)),
                  _resident((R_HEADS, CHUNK, CHUNK)), _resident((R_HEADS, CHUNK, R_V)),
                  _resident((SUBLANES, CHUNK)), _resident((SUBLANES, R_V)), _resident((SUBLANES, R_WIDTH)),
                  _resident((M_WIDTH, D_MODEL)), _resident((A_WIDTH, D_MODEL)),
                  _resident((R_WIDTH, D_MODEL)), _resident((D_MODEL, D_MODEL)),
                  _resident((SUBLANES, D_MODEL)),
                  _resident((D_MODEL, D_FF)), _resident((D_MODEL, D_FF)),
                  _resident((D_FF, D_MODEL)), _resident((SUBLANES, D_MODEL))],
        out_specs=mspec(D_MODEL),
        out_shape=jax.ShapeDtypeStruct((t, D_MODEL), F32),
        scratch_shapes=[pltpu.VMEM((2, tm, M_WIDTH), BF16),
                        pltpu.VMEM((2, tm, R_WIDTH), BF16),
                        pltpu.VMEM((SUBLANES + tm, w512), F32),
                        pltpu.VMEM((M_HEADS, M_QK, 2 * M_V), F32),
                        pltpu.VMEM((SUBLANES, LANES), F32),
                        pltpu.VMEM((R_HEADS, R_QK, R_V), F32)],
        compiler_params=_cparams(("arbitrary",), VMEM_LIMIT_TAIL),
        name="tail",
    )(x2, ha, zb, zb, zb, zf, zf, zb, zb, zf, zf, zb, zb, cosf, sinf,
      conv_w, gbias, mng, intra, qdec, kdec, cdec, rng, pm, pa, pr, wo, n2, wg, wu, wd, fg)


def _pad_rows(v, rows=SUBLANES):
    v = jnp.atleast_2d(v).astype(F32)
    return jnp.pad(v, ((0, rows - v.shape[0]), (0, 0)))


def _t5_bucket(dist):
    max_exact = N_BUCKETS // 2
    d_f = jnp.maximum(dist, 1).astype(F32)
    large = max_exact + (jnp.log(d_f / max_exact) / math.log(MAX_DISTANCE / max_exact)
                         * (N_BUCKETS - max_exact)).astype(I32)
    large = jnp.minimum(large, N_BUCKETS - 1)
    return jnp.where(dist < max_exact, dist, large)


def _bias_table(rel_bias):
    d = jnp.arange(N_BIAS_TABLES, dtype=I32)[:, None, None]
    j = jnp.arange(KEY_TILE, dtype=I32)[None, :, None]
    i = jnp.arange(Q_BLOCK, dtype=I32)[None, None, :]
    bucket = _t5_bucket(jnp.maximum(i + d * Q_BLOCK - j, 0))
    onehot = (bucket[None] == jnp.arange(N_BUCKETS, dtype=I32)[:, None, None, None]).astype(F32)
    return jnp.einsum("nh,ndkq->dhkq", rel_bias.astype(F32), onehot, precision=lax.Precision.HIGHEST)


def _rotary_tables(s):
    half = R_QK // 2
    freqs = ROPE_BASE ** (-jnp.linspace(0.0, 1.0, half, dtype=F32))
    ang = jnp.arange(s, dtype=F32)[:, None] * freqs[None, :]
    cos, sin = jnp.cos(ang), jnp.sin(ang)
    return jnp.concatenate([cos, cos], axis=1), jnp.concatenate([-sin, sin], axis=1)


def _retention_tables():
    h = R_HEADS
    log_gamma = jnp.log1p(-jnp.exp2(-5.0 - jnp.arange(h, dtype=F32)))
    pos = jnp.arange(CHUNK, dtype=F32)
    diff = pos[:, None] - pos[None, :]
    intra = jnp.where(diff >= 0, jnp.exp(jnp.maximum(diff, 0.0)[None] * log_gamma[:, None, None]), 0.0)
    q_decay = jnp.exp((pos[None, :] + 1.0) * log_gamma[:, None])
    k_decay = jnp.exp((CHUNK - 1.0 - pos[None, :]) * log_gamma[:, None])
    chunk_decay = jnp.exp(CHUNK * log_gamma)
    qdec = jnp.broadcast_to(q_decay[:, :, None], (h, CHUNK, R_V))
    kdec = _pad_rows(k_decay)
    cdec = _pad_rows(jnp.broadcast_to(chunk_decay[:, None], (h, R_V)))
    return intra, qdec, kdec, cdec


def _layout_w_in(w_in):
    wb = w_in.astype(BF16)
    parts = jnp.split(wb, [int(c) for c in np.cumsum(IN_SPLITS)[:-1]], axis=-1)
    (m_q, m_k, m_v, m_i, m_f, m_o, a_cq, a_kidx, a_widx, a_ckv,
     r_q, r_k, r_v, r_g, g_m, g_a, g_r) = parts
    used = IDX_DIM + 2 * M_HEADS + IDX_HEADS
    pad = jnp.zeros(wb.shape[:-1] + (LANES - used,), BF16)
    w = jnp.concatenate([m_q, m_k, a_cq, a_ckv, a_kidx, m_i, m_f, a_widx, pad, r_q, r_k,
                         m_v, m_o, r_v, r_g, g_m, g_a, g_r], axis=-1)
    assert w.shape[-1] == ZF_WIDTH + ZB_WIDTH
    return w


def _row_tile(t, want):
    while t % want:
        want //= 2
    return want


def kernel(x, rel_bias, final_norm_g, norm1_g, w_in, m_conv, m_ibias, m_fbias, m_norm_g, a_qnorm_g, a_wuq, a_wuq_idx, a_kidx_g, a_kvnorm_g, a_wuk, a_wuv, r_norm_g, p_m, p_a, p_r, w_out, norm2_g, w_gate, w_up, w_down):
    b, s, d = x.shape
    depth = w_in.shape[0]
    assert d == D_MODEL and s % KEY_TILE == 0 and s % TAIL_TILE == 0
    assert SM_F == SM_I + M_HEADS and M_V == LANES and R_V == LANES and CHUNK == LANES
    t = b * s
    tm_in = _row_tile(t, 2048)
    tm = _row_tile(t, 512)

    bias_tab = _bias_table(rel_bias)
    cosf, sinf = _rotary_tables(s)
    intra, qdec, kdec, cdec = _retention_tables()
    fg = _pad_rows(final_norm_g)

    w_in_l = _layout_w_in(w_in)
    x2 = x.reshape(t, d)
    for l in range(depth):
        zf, zb = _inproj(x2, norm1_g[l][None, :], w_in_l[l], tm_in, 1024)

        gbias = jnp.zeros((LANES,), F32)
        gbias = gbias.at[SM_I:SM_I + M_HEADS].set(m_ibias[l]).at[SM_F:SM_F + M_HEADS].set(m_fbias[l])

        wuqi_t = a_wuq_idx[l].reshape(Q_RANK, IDX_HEADS, IDX_DIM).transpose(1, 2, 0)
        wuqi_t = jnp.pad(wuqi_t, ((0, 0), (0, LANES - IDX_DIM), (0, 0))).astype(BF16)
        kig = jnp.pad(a_kidx_g[l], (0, LANES - IDX_DIM))
        qlat_t, qidx_t, widx, kidx, ckv, ckv_t = _dsa_prep(
            zf, _pad_rows(a_qnorm_g[l]), _pad_rows(a_kvnorm_g[l]), _pad_rows(kig),
            a_wuq[l].T.astype(BF16), wuqi_t, jnp.swapaxes(a_wuk[l], 1, 2).astype(BF16), tm)
        ha = _dsa(qidx_t, qlat_t, widx, kidx, ckv, ckv_t, bias_tab,
                  jnp.swapaxes(a_wuv[l], 1, 2).astype(BF16), b, s)

        x2 = _tail(x2, ha, zf, zb, cosf, sinf, _pad_rows(m_conv[l]), _pad_rows(gbias),
                   _pad_rows(m_norm_g[l]), intra, qdec, kdec, cdec, _pad_rows(r_norm_g[l]),
                   p_m[l].astype(BF16), p_a[l].astype(BF16), p_r[l].astype(BF16),
                   w_out[l].astype(BF16), _pad_rows(norm2_g[l]), w_gate[l].astype(BF16),
                   w_up[l].astype(BF16), w_down[l].astype(BF16), fg, s, final=(l == depth - 1))
    return x2.reshape(b, s, d)
```

```python
import functools
import math

import numpy as np
import jax
import jax.numpy as jnp
from jax import lax
from jax.experimental import pallas as pl
from jax.experimental.pallas import tpu as pltpu

F32 = jnp.float32
BF16 = jnp.bfloat16
I32 = jnp.int32

D_MODEL = 1024
M_HEADS, M_QK, M_V = 4, 64, 128
M_WIDTH = M_HEADS * M_V
CONV_W = 4
A_HEADS, A_HEAD_DIM = 8, 64
A_WIDTH = A_HEADS * A_HEAD_DIM
Q_RANK, KV_RANK = 256, 128
IDX_HEADS, IDX_DIM = 8, 64
TOPK_MAX = 256
R_HEADS, R_QK, R_V = 4, 128, 128
R_WIDTH = R_HEADS * R_V
CHUNK = 128
Q_BLOCK = 256
N_BUCKETS = 32
MAX_DISTANCE = 128
ROPE_BASE = 10000.0
EPS = 1e-6
D_FF = -(((-8 * D_MODEL) // 3) // 256) * 256
IN_SPLITS = (M_HEADS * M_QK, M_HEADS * M_QK, M_WIDTH, M_HEADS, M_HEADS, M_WIDTH,
             Q_RANK, IDX_DIM, IDX_HEADS, KV_RANK,
             R_HEADS * R_QK, R_HEADS * R_QK, R_WIDTH, R_WIDTH,
             D_MODEL, D_MODEL, D_MODEL)

LANES = 128
SUBLANES = 8
VMEM_LIMIT = 56 * 1024 * 1024
VMEM_LIMIT_TAIL = 62 * 1024 * 1024

F_MQK = 0
F_ACQ = 512
F_ACKV = 768
F_SMALL = 896
F_RQ, F_RK = 1024, 1536
ZF_WIDTH = 2048
B_MV, B_MO, B_RV, B_RG = 0, 512, 1024, 1536
B_GM, B_GA, B_GR = 2048, 3072, 4096
ZB_WIDTH = 5120
SM_KIDX, SM_I, SM_F, SM_W = 0, 64, 68, 72

KEY_TILE = 256
N_BIAS_TABLES = -(-(MAX_DISTANCE + KEY_TILE - 1) // Q_BLOCK) + 1
INT_MIN = -2 ** 31


def _cparams(sem, vmem_limit=VMEM_LIMIT):
    return pltpu.CompilerParams(dimension_semantics=sem, vmem_limit_bytes=vmem_limit)


def _resident(shape):
    nd = len(shape)
    return pl.BlockSpec(shape, lambda *_: (0,) * nd, pipeline_mode=pl.Buffered(1))


def _rms(x, g):
    return x * lax.rsqrt(jnp.mean(x * x, axis=-1, keepdims=True) + EPS) * g


def _group_norm(x, g):
    mu = jnp.mean(x, axis=-1, keepdims=True)
    xc = x - mu
    return xc * lax.rsqrt(jnp.mean(xc * xc, axis=-1, keepdims=True) + EPS) * g


def _dot(a, b):
    return jnp.dot(a, b, preferred_element_type=F32)


def _inproj_kernel(x_ref, g_ref, w_ref, zf_ref, zb_ref, h_scr, *, nf):
    j = pl.program_id(1)

    @pl.when(j == 0)
    def _():
        h_scr[...] = _rms(x_ref[...], g_ref[...]).astype(BF16)

    @pl.when(j < nf)
    def _():
        zf_ref[...] = _dot(h_scr[...], w_ref[...])

    @pl.when(j >= nf)
    def _():
        zb_ref[...] = _dot(h_scr[...], w_ref[...]).astype(BF16)


def _inproj(x2, g, w, tm, tn):
    t = x2.shape[0]
    nf = ZF_WIDTH // tn
    return pl.pallas_call(
        functools.partial(_inproj_kernel, nf=nf),
        grid=(t // tm, (ZF_WIDTH + ZB_WIDTH) // tn),
        in_specs=[pl.BlockSpec((tm, D_MODEL), lambda i, j: (i, 0)),
                  pl.BlockSpec((1, D_MODEL), lambda i, j: (0, 0)),
                  pl.BlockSpec((D_MODEL, tn), lambda i, j: (0, j))],
        out_specs=[pl.BlockSpec((tm, tn), lambda i, j: (i, jnp.minimum(j, nf - 1))),
                   pl.BlockSpec((tm, tn), lambda i, j: (i, jnp.maximum(j - nf, 0)))],
        out_shape=[jax.ShapeDtypeStruct((t, ZF_WIDTH), F32),
                   jax.ShapeDtypeStruct((t, ZB_WIDTH), BF16)],
        scratch_shapes=[pltpu.VMEM((tm, D_MODEL), BF16)],
        compiler_params=_cparams(("parallel", "arbitrary")),
        name="inproj",
    )(x2, g, w)


def _dsa_prep_kernel(cq_ref, ckv_ref, sm_ref, qn_ref, kvn_ref, kig_ref, wuq_t_ref, wuqi_t_ref,
                     wuk_t_ref, qlat_ref, qidx_ref, widx_ref, kidx_ref, ckvn_ref, ckvt_ref):
    tm = cq_ref.shape[0]
    c_qt = _rms(cq_ref[...], qn_ref[0:1, :]).T.astype(BF16)
    q_t = _dot(wuq_t_ref[...], c_qt).astype(BF16)
    for h in range(A_HEADS):
        ql_t = (_dot(wuk_t_ref[h], q_t[h * A_HEAD_DIM:(h + 1) * A_HEAD_DIM, :])
                * (A_HEAD_DIM ** -0.5)).astype(BF16)
        qi_t = _dot(wuqi_t_ref[h], c_qt).astype(BF16)
        for j in range(tm // Q_BLOCK):
            cols = slice(j * Q_BLOCK, (j + 1) * Q_BLOCK)
            qlat_ref[j, :, h * Q_BLOCK:(h + 1) * Q_BLOCK] = ql_t[:, cols]
            qidx_ref[j, :, h * Q_BLOCK:(h + 1) * Q_BLOCK] = qi_t[:, cols]
    sm = sm_ref[...]
    w_t = sm.T[SM_W:SM_W + IDX_HEADS, :] * (IDX_HEADS ** -0.5 * IDX_DIM ** -0.5)
    for j in range(tm // Q_BLOCK):
        widx_ref[j] = w_t[:, j * Q_BLOCK:(j + 1) * Q_BLOCK]
    lane = lax.broadcasted_iota(I32, sm.shape, 1)
    inside = lane < IDX_DIM
    mu = jnp.sum(jnp.where(inside, sm, 0.0), axis=-1, keepdims=True) * (1.0 / IDX_DIM)
    xc = jnp.where(inside, sm - mu, 0.0)
    var = jnp.sum(xc * xc, axis=-1, keepdims=True) * (1.0 / IDX_DIM)
    kidx_ref[...] = (xc * lax.rsqrt(var + EPS) * kig_ref[0:1, :]).astype(BF16)
    ckvn = _rms(ckv_ref[...], kvn_ref[0:1, :])
    ckvn_ref[...] = ckvn.astype(BF16)
    ckvn_t = ckvn.T.astype(BF16)
    for j in range(tm // KEY_TILE):
        ckvt_ref[j] = ckvn_t[:, j * KEY_TILE:(j + 1) * KEY_TILE]


def _dsa_prep(z, qn, kvn, kig, wuq_t, wuqi_t, wuk_t, tm):
    t = z.shape[0]
    hq = A_HEADS * Q_BLOCK
    nqb = tm // Q_BLOCK
    qspec = pl.BlockSpec((nqb, LANES, hq), lambda i: (i, 0, 0))
    return pl.pallas_call(
        _dsa_prep_kernel,
        grid=(t // tm,),
        in_specs=[pl.BlockSpec((tm, Q_RANK), lambda i: (i, F_ACQ // Q_RANK)),
                  pl.BlockSpec((tm, KV_RANK), lambda i: (i, F_ACKV // KV_RANK)),
                  pl.BlockSpec((tm, LANES), lambda i: (i, F_SMALL // LANES)),
                  _resident((SUBLANES, Q_RANK)),
                  _resident((SUBLANES, KV_RANK)),
                  _resident((SUBLANES, LANES)),
                  _resident((A_WIDTH, Q_RANK)),
                  _resident((IDX_HEADS, LANES, Q_RANK)),
                  _resident((A_HEADS, KV_RANK, A_HEAD_DIM))],
        out_specs=[qspec, qspec,
                   pl.BlockSpec((nqb, IDX_HEADS, Q_BLOCK), lambda i: (i, 0, 0)),
                   pl.BlockSpec((tm, LANES), lambda i: (i, 0)),
                   pl.BlockSpec((tm, KV_RANK), lambda i: (i, 0)),
                   pl.BlockSpec((tm // KEY_TILE, KV_RANK, KEY_TILE), lambda i: (i, 0, 0))],
        out_shape=[jax.ShapeDtypeStruct((t // Q_BLOCK, KV_RANK, hq), BF16),
                   jax.ShapeDtypeStruct((t // Q_BLOCK, LANES, hq), BF16),
                   jax.ShapeDtypeStruct((t // Q_BLOCK, IDX_HEADS, Q_BLOCK), F32),
                   jax.ShapeDtypeStruct((t, LANES), BF16),
                   jax.ShapeDtypeStruct((t, KV_RANK), BF16),
                   jax.ShapeDtypeStruct((t // KEY_TILE, KV_RANK, KEY_TILE), BF16)],
        compiler_params=_cparams(("parallel",)),
        name="dsa_prep",
    )(z, z, z, qn, kvn, kig, wuq_t, wuqi_t, wuk_t)


def _fold(x, op, rows=SUBLANES):
    parts = [x[i:i + rows] for i in range(0, x.shape[0], rows)]
    while len(parts) > 1:
        parts = [op(parts[i], parts[i + 1]) for i in range(0, len(parts), 2)]
    return parts[0]


def _tile_loop(n, body, init, quads=False):
    def pair(j, carry):
        return body(j + 1, body(j, carry))

    def quad(j, carry):
        return pair(4 * j + 2, pair(4 * j, carry))

    start = 0
    carry = init
    if quads:
        carry = lax.fori_loop(0, n // 4, quad, carry)
        start = 4 * (n // 4)
    npair = (n - start) // 2
    carry = lax.fori_loop(0, npair, lambda j, c: pair(start + 2 * j, c), carry)
    return lax.fori_loop(start + 2 * npair, n, body, carry)


def _dsa_kernel(qidx_ref, qlat_ref, widx_ref, kidx_ref, ckv_ref, ckvt_ref, bias_ref, wuv_t_ref,
                out_ref, keys_scr, lg_scr, acc_scr, *, top_k):
    QB = Q_BLOCK
    KT = KEY_TILE
    qblk = pl.program_id(1)
    nk = (qblk * QB + QB + KT - 1) // KT
    kf = float(top_k)
    hcols = [slice(h * QB, (h + 1) * QB) for h in range(A_HEADS)]
    hrows = [slice(h * SUBLANES, (h + 1) * SUBLANES) for h in range(A_HEADS)]

    q_pos = qblk * QB + lax.broadcasted_iota(I32, (KT, QB), 1)
    key_off = lax.broadcasted_iota(I32, (KT, QB), 0)

    qi_t = qidx_ref[0]
    w_rows = widx_ref[0]

    def score_body(kt, carry):
        s_t = _dot(kidx_ref[0, kt], qi_t)
        tot = jnp.zeros((KT, QB), F32)
        for g in range(IDX_HEADS):
            tot = tot + w_rows[g:g + 1, :] * jnp.maximum(s_t[:, hcols[g]], 0.0)
        key_pos = kt * KT + key_off
        tot = jnp.where(key_pos <= q_pos, tot + 0.0, -jnp.inf)
        bits = pltpu.bitcast(tot, I32)
        keys_scr[kt] = bits ^ ((bits >> 31) & 0x7FFFFFFF)
        return carry

    _tile_loop(nk, score_body, 0, quads=True)

    def count(pred):
        def body(kt, acc):
            return acc + _fold(jnp.where(pred(keys_scr[kt], kt), 1.0, 0.0), jnp.add)
        acc = _tile_loop(nk, body, jnp.zeros((SUBLANES, QB), F32))
        return jnp.sum(acc, axis=0, keepdims=True)

    def count_ge(cand):
        return count(lambda keys, kt: keys >= cand)

    n_bits = jnp.where((qblk + 1) * QB <= top_k, 0, 31)
    thr = jnp.where((count_ge(jnp.zeros((1, QB), I32)) >= kf) & (n_bits > 0), 0, INT_MIN).astype(I32)

    def bit_body(i, thr):
        cand = thr | (jnp.int32(1) << (30 - i))
        return jnp.where(count_ge(cand) >= kf, cand, thr)

    thr = lax.fori_loop(0, n_bits, bit_body, thr)

    over = jnp.max(jnp.where(count_ge(thr) > kf, 1.0, 0.0))

    @pl.when(over > 0.0)
    def _():
        need = kf - count(lambda keys, kt: keys > thr)
        r_k = lax.broadcasted_iota(I32, (KT, KT), 0)
        c_k = lax.broadcasted_iota(I32, (KT, KT), 1)
        lower = jnp.where(r_k >= c_k, 1.0, 0.0).astype(BF16)

        def demote(kt, before):
            keys = keys_scr[kt]
            tie = keys == thr
            rank = _dot(lower, jnp.where(tie, 1.0, 0.0).astype(BF16)) + before
            keys_scr[kt] = jnp.where(tie & (rank > need), thr - 1, keys)
            return rank[KT - 1:KT, :]

        _tile_loop(nk, demote, jnp.zeros((1, QB), F32))

    ql_t = qlat_ref[0]

    def pass_a(kt, mx):
        lg_t = _dot(ckv_ref[0, kt], ql_t)
        keys = keys_scr[kt]
        key_pos = kt * KT + key_off
        sel = (keys >= thr) & (key_pos <= q_pos)
        didx = jnp.minimum((qblk * QB - kt * KT) // QB, N_BIAS_TABLES - 1)
        new = []
        for h in range(A_HEADS):
            lh = jnp.where(sel, lg_t[:, hcols[h]] + bias_ref[didx, h], -jnp.inf)
            lg_scr[kt, :, hcols[h]] = lh
            new.append(jnp.maximum(mx[hrows[h]], _fold(lh, jnp.maximum)))
        return jnp.concatenate(new, axis=0)

    mx = _tile_loop(nk, pass_a, jnp.full((A_HEADS * SUBLANES, QB), -jnp.inf, F32), quads=True)
    m_rows = [jnp.max(mx[hrows[h]], axis=0, keepdims=True) for h in range(A_HEADS)]

    acc_scr[...] = jnp.zeros_like(acc_scr)

    def pass_b(kt, l):
        lg = lg_scr[kt]
        ps, new = [], []
        for h in range(A_HEADS):
            p = jnp.exp(lg[:, hcols[h]] - m_rows[h])
            new.append(l[hrows[h]] + _fold(p, jnp.add))
            ps.append(p.astype(BF16))
        acc_scr[...] = acc_scr[...] + _dot(ckvt_ref[0, kt], jnp.concatenate(ps, axis=1))
        return jnp.concatenate(new, axis=0)

    l = _tile_loop(nk, pass_b, jnp.zeros((A_HEADS * SUBLANES, QB), F32), quads=True)

    outs = []
    for h in range(A_HEADS):
        o_t = (acc_scr[:, hcols[h]] / jnp.sum(l[hrows[h]], axis=0, keepdims=True)).astype(BF16)
        outs.append(_dot(wuv_t_ref[h], o_t))
    out_ref[...] = jnp.concatenate(outs, axis=0).T.astype(BF16)


def _dsa(qidx_t, qlat_t, widx, kidx, ckv, ckv_t, bias_tab, wuv_t, b, s):
    nq = s // Q_BLOCK
    nt = s // KEY_TILE
    top_k = min(TOPK_MAX, s // 4)
    hq = A_HEADS * Q_BLOCK
    row = lambda bi, qi: bi * nq + qi
    qspec = pl.BlockSpec((1, LANES, hq), lambda bi, qi: (row(bi, qi), 0, 0))
    kspec = pl.BlockSpec((1, nt, KEY_TILE, LANES), lambda bi, qi: (bi, 0, 0, 0))
    return pl.pallas_call(
        functools.partial(_dsa_kernel, top_k=top_k),
        grid=(b, nq),
        in_specs=[qspec, qspec,
                  pl.BlockSpec((1, IDX_HEADS, Q_BLOCK), lambda bi, qi: (row(bi, qi), 0, 0)),
                  kspec, kspec,
                  pl.BlockSpec((1, nt, KV_RANK, KEY_TILE), lambda bi, qi: (bi, 0, 0, 0)),
                  _resident((N_BIAS_TABLES, A_HEADS, KEY_TILE, Q_BLOCK)),
                  _resident((A_HEADS, A_HEAD_DIM, KV_RANK))],
        out_specs=pl.BlockSpec((Q_BLOCK, A_WIDTH), lambda bi, qi: (row(bi, qi), 0)),
        out_shape=jax.ShapeDtypeStruct((b * s, A_WIDTH), BF16),
        scratch_shapes=[pltpu.VMEM((nt, KEY_TILE, Q_BLOCK), I32),
                        pltpu.VMEM((nt, KEY_TILE, hq), F32),
                        pltpu.VMEM((KV_RANK, hq), F32)],
        compiler_params=_cparams(("parallel", "arbitrary")),
        name="dsa",
    )(qidx_t, qlat_t, widx, kidx.reshape(b, nt, KEY_TILE, LANES), ckv.reshape(b, nt, KEY_TILE, LANES),
      ckv_t.reshape(b, nt, KV_RANK, KEY_TILE), bias_tab, wuv_t)


FF_CHUNK = 256
TAIL_TILE = 512


def _mlstm_chunk(qk, sm, v_ref, o_ref, rows, gbias, ng_ref, consts, state):
    L = CHUNK
    causal, lower, upper, is_f, ones = consts
    c_st, m_st = state
    hs = range(M_HEADS)
    exact = dict(preferred_element_type=F32, precision=lax.Precision.HIGHEST)
    kt_all = qk[:, M_HEADS * M_QK:].T
    g = sm + gbias
    xg = jnp.where(is_f, jax.nn.log_sigmoid(g), g)
    rows_if = xg.T[SM_I:SM_I + 2 * M_HEADS, :]
    cum_r = jnp.dot(rows_if, upper, **exact)
    bc = jnp.dot(lower, xg, **exact)
    bc_q = [jnp.broadcast_to(bc[:, SM_F + h:SM_F + h + 1], (L, LANES)) for h in hs]
    b_r = [rows_if[h:h + 1, :] - cum_r[M_HEADS + h:M_HEADS + h + 1, :] for h in hs]
    b_last = [bc_q[h][L - 1:L, :] for h in hs]
    d_log = [jnp.where(causal, bc_q[h] + b_r[h], -jnp.inf) for h in hs]
    inter = [bc_q[h] + m_st[h] for h in hs]
    m_row = [jnp.maximum(inter[h], jnp.max(d_log[h], axis=-1, keepdims=True)) for h in hs]
    w_intra = [jnp.exp(d_log[h] - m_row[h]) for h in hs]
    w_inter = [jnp.exp(inter[h] - m_row[h]) for h in hs]
    floor = [jnp.exp(-m_row[h]) for h in hs]

    qb = [(qk[:, h * M_QK:(h + 1) * M_QK] * (M_QK ** -0.5)).astype(BF16) for h in hs]
    kt = [kt_all[h * M_QK:(h + 1) * M_QK, :] for h in hs]
    vaug = [jnp.concatenate([v_ref[rows, h * M_V:(h + 1) * M_V], ones], axis=1) for h in hs]
    sc = [_dot(qb[h], kt[h].astype(BF16)) for h in hs]
    cross = [_dot(qb[h], c_st[h].astype(BF16)) for h in hs]

    g_r = [b_last[h] + b_r[h] for h in hs]
    m_new = [jnp.maximum(b_last[h] + m_st[h], jnp.max(g_r[h], axis=-1, keepdims=True)) for h in hs]
    wk_r = [jnp.exp(g_r[h] - m_new[h]) for h in hs]
    decay = [jnp.exp(b_last[h] + m_st[h] - m_new[h]) for h in hs]
    upd = [_dot((kt[h] * wk_r[h]).astype(BF16), vaug[h]) for h in hs]

    sc = [(sc[h] * w_intra[h]).astype(BF16) for h in hs]
    num = [_dot(sc[h], vaug[h]) + jnp.concatenate([w_inter[h], w_inter[h]], axis=1) * cross[h] for h in hs]
    out = [num[h][:, :M_V] / jnp.maximum(jnp.abs(num[h][:, M_V:]), floor[h]) for h in hs]
    c_new = [jnp.concatenate([decay[h], decay[h]], axis=1) * c_st[h] + upd[h] for h in hs]
    y = [_group_norm(out[h], ng_ref[0:1, h * M_V:(h + 1) * M_V]) for h in hs]
    hm = [(y[h] * jax.nn.sigmoid(o_ref[rows, h * M_V:(h + 1) * M_V].astype(F32))).astype(BF16) for h in hs]
    return jnp.concatenate(hm, axis=1), (c_new, m_new)


def _ret_chunk(q_ref, k_ref, v_ref, g_ref, rows, cosf, sinf, intra_ref, qdec_ref, kdec_ref, cdec_ref,
               ng_ref, state):
    hs = range(R_HEADS)
    sl = [slice(h * R_QK, (h + 1) * R_QK) for h in hs]
    half = R_QK // 2

    def rot(x):
        return x * cosf + pltpu.roll(x, half, 1) * sinf

    qr = [rot(q_ref[rows, sl[h]]).astype(BF16) for h in hs]
    kt = [(rot(k_ref[rows, sl[h]]) * (R_QK ** -0.5)).T for h in hs]
    vb = [v_ref[rows, sl[h]] for h in hs]
    sc = [_dot(qr[h], kt[h].astype(BF16)) for h in hs]
    cross = [_dot(qr[h], state[h].astype(BF16)) for h in hs]
    upd = [_dot((kt[h] * kdec_ref[h:h + 1, :]).astype(BF16), vb[h]) for h in hs]
    sc = [(sc[h] * intra_ref[h]).astype(BF16) for h in hs]
    out = [_dot(sc[h], vb[h]) + qdec_ref[h] * cross[h] for h in hs]
    new = [cdec_ref[h:h + 1, :] * state[h] + upd[h] for h in hs]
    y = [_group_norm(out[h], ng_ref[0:1, sl[h]]) for h in hs]
    hr = []
    for h in hs:
        gt = g_ref[rows, sl[h]].astype(F32)
        hr.append((y[h] * (gt * jax.nn.sigmoid(gt))).astype(BF16))
    return jnp.concatenate(hr, axis=1), new


def _tail_kernel(x_ref, ha_ref, gm_ref, ga_ref, gr_ref,
                 qk_ref, sm_ref, mv_ref, mo_ref, rq_ref, rk_ref, rv_ref, rg_ref, cos_ref, sin_ref,
                 conv_ref, gb_ref, mng_ref, intra_ref, qdec_ref, kdec_ref, cdec_ref, rng_ref,
                 pm_ref, pa_ref, pr_ref, wo_ref, n2_ref, wg_ref, wu_ref, wd_ref, fg_ref,
                 out_ref, hm_scr, hr_scr, xbuf, cst, mst, rst, *, final, n_tiles, tiles_per_row):
    L = CHUNK
    TM = TAIL_TILE
    i = pl.program_id(0)
    scan_tile = jnp.minimum(i, n_tiles - 1)
    cur = lax.rem(i, 2)
    prev = lax.rem(i + 1, 2)

    @pl.when(i == 0)
    def _():
        hm_scr[...] = jnp.zeros_like(hm_scr)
        hr_scr[...] = jnp.zeros_like(hr_scr)

    @pl.when(lax.rem(scan_tile, tiles_per_row) == 0)
    def _():
        xbuf[0:SUBLANES, :] = jnp.zeros((SUBLANES, 2 * M_HEADS * M_QK), F32)
        cst[...] = jnp.zeros_like(cst)
        mst[...] = jnp.zeros_like(mst)
        rst[...] = jnp.zeros_like(rst)

    y = (jax.nn.sigmoid(gm_ref[...].astype(F32)) * _dot(hm_scr[prev], pm_ref[...])
         + jax.nn.sigmoid(ga_ref[...].astype(F32)) * _dot(ha_ref[...], pa_ref[...])
         + jax.nn.sigmoid(gr_ref[...].astype(F32)) * _dot(hr_scr[prev], pr_ref[...]))
    x1 = x_ref[...] + _dot(y.astype(BF16), wo_ref[...])
    h2 = _rms(x1, n2_ref[0:1, :]).astype(BF16)

    x_qk = qk_ref[...]
    xbuf[SUBLANES:SUBLANES + TM, :] = x_qk
    w = conv_ref[...]
    conv = w[CONV_W - 1:CONV_W, :] * x_qk
    for j in range(CONV_W - 1):
        off = SUBLANES - (CONV_W - 1) + j
        conv = conv + w[j:j + 1, :] * xbuf[off:off + TM, :]
    xbuf[0:SUBLANES, :] = x_qk[TM - SUBLANES:TM, :]
    qk_all = conv * jax.nn.sigmoid(conv)

    r_i = lax.broadcasted_iota(I32, (L, L), 0)
    c_i = lax.broadcasted_iota(I32, (L, L), 1)
    lane = lax.broadcasted_iota(I32, (L, LANES), 1)
    causal = r_i >= c_i
    consts = (causal, causal.astype(F32), (r_i <= c_i).astype(F32),
              (lane >= SM_F) & (lane < SM_F + M_HEADS), jnp.ones((L, M_V), BF16))
    gbias = gb_ref[0:1, :]
    m_state = ([cst[h] for h in range(M_HEADS)], [mst[h:h + 1, :] for h in range(M_HEADS)])
    r_state = [rst[h] for h in range(R_HEADS)]

    def scan_unit(u, m_state, r_state):
        c = u // 2
        rows = slice(c * L, (c + 1) * L)
        if u % 2 == 0:
            hm, m_state = _mlstm_chunk(qk_all[rows], sm_ref[rows, :], mv_ref, mo_ref, rows, gbias,
                                       mng_ref, consts, m_state)
            hm_scr[cur, rows, :] = hm
        else:
            hr, r_state = _ret_chunk(rq_ref, rk_ref, rv_ref, rg_ref, rows, cos_ref[rows, :],
                                     sin_ref[rows, :], intra_ref, qdec_ref, kdec_ref, cdec_ref,
                                     rng_ref, r_state)
            hr_scr[cur, rows, :] = hr
        return m_state, r_state

    n_units = 2 * (TM // L)
    n_ff = D_FF // FF_CHUNK
    assert n_units <= n_ff
    acc = x1
    for c in range(n_ff):
        cs = slice(c * FF_CHUNK, (c + 1) * FF_CHUNK)
        gate = _dot(h2, wg_ref[:, cs])
        up = _dot(h2, wu_ref[:, cs])
        act = (gate * jax.nn.sigmoid(gate) * up).astype(BF16)
        acc = acc + _dot(act, wd_ref[cs, :])
        if c < n_units:
            m_state, r_state = scan_unit(c, m_state, r_state)

    for h in range(M_HEADS):
        cst[h] = m_state[0][h]
    mst[0:M_HEADS, :] = jnp.concatenate(m_state[1], axis=0)
    for h in range(R_HEADS):
        rst[h] = r_state[h]
    if final:
        acc = _rms(acc, fg_ref[0:1, :])
    out_ref[...] = acc


def _tail(x2, ha, zf, zb, cosf, sinf, conv_w, gbias, mng, intra, qdec, kdec, cdec, rng,
          pm, pa, pr, wo, n2, wg, wu, wd, fg, s, final):
    t = x2.shape[0]
    tm = TAIL_TILE
    n_tiles = t // tm
    tiles_per_row = s // tm
    w512 = 2 * M_HEADS * M_QK
    merge_tile = lambda i: jnp.maximum(i - 1, 0)
    scan_tile = lambda i: jnp.minimum(i, n_tiles - 1)
    mspec = lambda w, cb=0: pl.BlockSpec((tm, w), lambda i: (merge_tile(i), cb))
    sspec = lambda w, off: pl.BlockSpec((tm, w), lambda i: (scan_tile(i), off // w))
    pspec = pl.BlockSpec((tm, R_QK), lambda i: (lax.rem(scan_tile(i), tiles_per_row), 0))
    return pl.pallas_call(
        functools.partial(_tail_kernel, final=final, n_tiles=n_tiles, tiles_per_row=tiles_per_row),
        grid=(n_tiles + 1,),
        in_specs=[mspec(D_MODEL), mspec(A_WIDTH),
                  mspec(D_MODEL, B_GM // D_MODEL), mspec(D_MODEL, B_GA // D_MODEL),
                  mspec(D_MODEL, B_GR // D_MODEL),
                  sspec(w512, F_MQK), sspec(LANES, F_SMALL), sspec(M_WIDTH, B_MV), sspec(M_WIDTH, B_MO),
                  sspec(R_WIDTH, F_RQ), sspec(R_WIDTH, F_RK), sspec(R_WIDTH, B_RV), sspec(R_WIDTH, B_RG),
                  pspec, pspec,
                  _resident((SUBLANES, w512)), _resident((SUBLANES, LANES)), _resident((SUBLANES, M_WIDTH)),
                  _resident((R_HEADS, CHUNK, CHUNK)), _resident((R_HEADS, CHUNK, R_V)),
                  _resident((SUBLANES, CHUNK)), _resident((SUBLANES, R_V)), _resident((SUBLANES, R_WIDTH)),
                  _resident((M_WIDTH, D_MODEL)), _resident((A_WIDTH, D_MODEL)),
                  _resident((R_WIDTH, D_MODEL)), _resident((D_MODEL, D_MODEL)),
                  _resident((SUBLANES, D_MODEL)),
                  _resident((D_MODEL, D_FF)), _resident((D_MODEL, D_FF)),
                  _resident((D_FF, D_MODEL)), _resident((SUBLANES, D_MODEL))],
        out_specs=mspec(D_MODEL),
        out_shape=jax.ShapeDtypeStruct((t, D_MODEL), F32),
        scratch_shapes=[pltpu.VMEM((2, tm, M_WIDTH), BF16),
                        pltpu.VMEM((2, tm, R_WIDTH), BF16),
                        pltpu.VMEM((SUBLANES + tm, w512), F32),
                        pltpu.VMEM((M_HEADS, M_QK, 2 * M_V), F32),
                        pltpu.VMEM((SUBLANES, LANES), F32),
                        pltpu.VMEM((R_HEADS, R_QK, R_V), F32)],
        compiler_params=_cparams(("arbitrary",), VMEM_LIMIT_TAIL),
        name="tail",
    )(x2, ha, zb, zb, zb, zf, zf, zb, zb, zf, zf, zb, zb, cosf, sinf,
      conv_w, gbias, mng, intra, qdec, kdec, cdec, rng, pm, pa, pr, wo, n2, wg, wu, wd, fg)


def _pad_rows(v, rows=SUBLANES):
    v = jnp.atleast_2d(v).astype(F32)
    return jnp.pad(v, ((0, rows - v.shape[0]), (0, 0)))


def _t5_bucket(dist):
    max_exact = N_BUCKETS // 2
    d_f = jnp.maximum(dist, 1).astype(F32)
    large = max_exact + (jnp.log(d_f / max_exact) / math.log(MAX_DISTANCE / max_exact)
                         * (N_BUCKETS - max_exact)).astype(I32)
    large = jnp.minimum(large, N_BUCKETS - 1)
    return jnp.where(dist < max_exact, dist, large)


def _bias_table(rel_bias):
    d = jnp.arange(N_BIAS_TABLES, dtype=I32)[:, None, None]
    j = jnp.arange(KEY_TILE, dtype=I32)[None, :, None]
    i = jnp.arange(Q_BLOCK, dtype=I32)[None, None, :]
    bucket = _t5_bucket(jnp.maximum(i + d * Q_BLOCK - j, 0))
    onehot = (bucket[None] == jnp.arange(N_BUCKETS, dtype=I32)[:, None, None, None]).astype(F32)
    return jnp.einsum("nh,ndkq->dhkq", rel_bias.astype(F32), onehot, precision=lax.Precision.HIGHEST)


def _rotary_tables(s):
    half = R_QK // 2
    freqs = ROPE_BASE ** (-jnp.linspace(0.0, 1.0, half, dtype=F32))
    ang = jnp.arange(s, dtype=F32)[:, None] * freqs[None, :]
    cos, sin = jnp.cos(ang), jnp.sin(ang)
    return jnp.concatenate([cos, cos], axis=1), jnp.concatenate([-sin, sin], axis=1)


def _retention_tables():
    h = R_HEADS
    log_gamma = jnp.log1p(-jnp.exp2(-5.0 - jnp.arange(h, dtype=F32)))
    pos = jnp.arange(CHUNK, dtype=F32)
    diff = pos[:, None] - pos[None, :]
    intra = jnp.where(diff >= 0, jnp.exp(jnp.maximum(diff, 0.0)[None] * log_gamma[:, None, None]), 0.0)
    q_decay = jnp.exp((pos[None, :] + 1.0) * log_gamma[:, None])
    k_decay = jnp.exp((CHUNK - 1.0 - pos[None, :]) * log_gamma[:, None])
    chunk_decay = jnp.exp(CHUNK * log_gamma)
    qdec = jnp.broadcast_to(q_decay[:, :, None], (h, CHUNK, R_V))
    kdec = _pad_rows(k_decay)
    cdec = _pad_rows(jnp.broadcast_to(chunk_decay[:, None], (h, R_V)))
    return intra, qdec, kdec, cdec


def _layout_w_in(w_in):
    parts = jnp.split(w_in, [int(c) for c in np.cumsum(IN_SPLITS)[:-1]], axis=-1)
    (m_q, m_k, m_v, m_i, m_f, m_o, a_cq, a_kidx, a_widx, a_ckv,
     r_q, r_k, r_v, r_g, g_m, g_a, g_r) = parts
    small = jnp.concatenate([a_kidx, m_i, m_f, a_widx], axis=-1)
    small = jnp.pad(small, ((0, 0), (0, LANES - small.shape[-1])))
    w = jnp.concatenate([m_q, m_k, a_cq, a_ckv, small, r_q, r_k,
                         m_v, m_o, r_v, r_g, g_m, g_a, g_r], axis=-1)
    assert w.shape[-1] == ZF_WIDTH + ZB_WIDTH
    return w.astype(BF16)


def _row_tile(t, want):
    while t % want:
        want //= 2
    return want


def kernel(x, rel_bias, final_norm_g, norm1_g, w_in, m_conv, m_ibias, m_fbias, m_norm_g, a_qnorm_g, a_wuq, a_wuq_idx, a_kidx_g, a_kvnorm_g, a_wuk, a_wuv, r_norm_g, p_m, p_a, p_r, w_out, norm2_g, w_gate, w_up, w_down):
    b, s, d = x.shape
    depth = w_in.shape[0]
    assert d == D_MODEL and s % KEY_TILE == 0 and s % TAIL_TILE == 0
    assert SM_F == SM_I + M_HEADS and M_V == LANES and R_V == LANES and CHUNK == LANES
    t = b * s
    tm_in = _row_tile(t, 2048)
    tm = _row_tile(t, 512)

    bias_tab = _bias_table(rel_bias)
    cosf, sinf = _rotary_tables(s)
    intra, qdec, kdec, cdec = _retention_tables()
    fg = _pad_rows(final_norm_g)

    x2 = x.reshape(t, d)
    for l in range(depth):
        zf, zb = _inproj(x2, norm1_g[l][None, :], _layout_w_in(w_in[l]), tm_in, 1024)

        gbias = jnp.zeros((LANES,), F32)
        gbias = gbias.at[SM_I:SM_I + M_HEADS].set(m_ibias[l]).at[SM_F:SM_F + M_HEADS].set(m_fbias[l])

        wuqi_t = a_wuq_idx[l].reshape(Q_RANK, IDX_HEADS, IDX_DIM).transpose(1, 2, 0)
        wuqi_t = jnp.pad(wuqi_t, ((0, 0), (0, LANES - IDX_DIM), (0, 0))).astype(BF16)
        kig = jnp.pad(a_kidx_g[l], (0, LANES - IDX_DIM))
        qlat_t, qidx_t, widx, kidx, ckv, ckv_t = _dsa_prep(
            zf, _pad_rows(a_qnorm_g[l]), _pad_rows(a_kvnorm_g[l]), _pad_rows(kig),
            a_wuq[l].T.astype(BF16), wuqi_t, jnp.swapaxes(a_wuk[l], 1, 2).astype(BF16), tm)
        ha = _dsa(qidx_t, qlat_t, widx, kidx, ckv, ckv_t, bias_tab,
                  jnp.swapaxes(a_wuv[l], 1, 2).astype(BF16), b, s)

        x2 = _tail(x2, ha, zf, zb, cosf, sinf, _pad_rows(m_conv[l]), _pad_rows(gbias),
                   _pad_rows(m_norm_g[l]), intra, qdec, kdec, cdec, _pad_rows(r_norm_g[l]),
                   p_m[l].astype(BF16), p_a[l].astype(BF16), p_r[l].astype(BF16),
                   w_out[l].astype(BF16), _pad_rows(norm2_g[l]), w_gate[l].astype(BF16),
                   w_up[l].astype(BF16), w_down[l].astype(BF16), fg, s, final=(l == depth - 1))
    return x2.reshape(b, s, d)
```

```python
import functools
import math

import numpy as np
import jax
import jax.numpy as jnp
from jax import lax
from jax.experimental import pallas as pl
from jax.experimental.pallas import tpu as pltpu

F32 = jnp.float32
BF16 = jnp.bfloat16
I32 = jnp.int32

D_MODEL = 1024
M_HEADS, M_QK, M_V = 4, 64, 128
M_WIDTH = M_HEADS * M_V
CONV_W = 4
A_HEADS, A_HEAD_DIM = 8, 64
A_WIDTH = A_HEADS * A_HEAD_DIM
Q_RANK, KV_RANK = 256, 128
IDX_HEADS, IDX_DIM = 8, 64
TOPK_MAX = 256
R_HEADS, R_QK, R_V = 4, 128, 128
R_WIDTH = R_HEADS * R_V
CHUNK = 128
Q_BLOCK = 256
N_BUCKETS = 32
MAX_DISTANCE = 128
ROPE_BASE = 10000.0
EPS = 1e-6
D_FF = -(((-8 * D_MODEL) // 3) // 256) * 256
IN_SPLITS = (M_HEADS * M_QK, M_HEADS * M_QK, M_WIDTH, M_HEADS, M_HEADS, M_WIDTH,
             Q_RANK, IDX_DIM, IDX_HEADS, KV_RANK,
             R_HEADS * R_QK, R_HEADS * R_QK, R_WIDTH, R_WIDTH,
             D_MODEL, D_MODEL, D_MODEL)

LANES = 128
SUBLANES = 8
VMEM_LIMIT = 56 * 1024 * 1024
VMEM_LIMIT_TAIL = 62 * 1024 * 1024

F_MQK = 0
F_ACQ = 512
F_ACKV = 768
F_SMALL = 896
F_RQ, F_RK = 1024, 1536
ZF_WIDTH = 2048
B_MV, B_MO, B_RV, B_RG = 0, 512, 1024, 1536
B_GM, B_GA, B_GR = 2048, 3072, 4096
ZB_WIDTH = 5120
SM_KIDX, SM_I, SM_F, SM_W = 0, 64, 68, 72

KEY_TILE = 256
N_BIAS_TABLES = -(-(MAX_DISTANCE + KEY_TILE - 1) // Q_BLOCK) + 1
INT_MIN = -2 ** 31


def _cparams(sem, vmem_limit=VMEM_LIMIT):
    return pltpu.CompilerParams(dimension_semantics=sem, vmem_limit_bytes=vmem_limit)


def _resident(shape):
    nd = len(shape)
    return pl.BlockSpec(shape, lambda *_: (0,) * nd, pipeline_mode=pl.Buffered(1))


def _rms(x, g):
    return x * lax.rsqrt(jnp.mean(x * x, axis=-1, keepdims=True) + EPS) * g


def _group_norm(x, g):
    mu = jnp.mean(x, axis=-1, keepdims=True)
    xc = x - mu
    return xc * lax.rsqrt(jnp.mean(xc * xc, axis=-1, keepdims=True) + EPS) * g


def _dot(a, b):
    return jnp.dot(a, b, preferred_element_type=F32)


def _inproj_kernel(x_ref, g_ref, w_ref, zf_ref, zb_ref, h_scr, *, nf):
    j = pl.program_id(1)

    @pl.when(j == 0)
    def _():
        h_scr[...] = _rms(x_ref[...], g_ref[...]).astype(BF16)

    @pl.when(j < nf)
    def _():
        zf_ref[...] = _dot(h_scr[...], w_ref[...])

    @pl.when(j >= nf)
    def _():
        zb_ref[...] = _dot(h_scr[...], w_ref[...]).astype(BF16)


def _inproj(x2, g, w, tm, tn):
    t = x2.shape[0]
    nf = ZF_WIDTH // tn
    return pl.pallas_call(
        functools.partial(_inproj_kernel, nf=nf),
        grid=(t // tm, (ZF_WIDTH + ZB_WIDTH) // tn),
        in_specs=[pl.BlockSpec((tm, D_MODEL), lambda i, j: (i, 0)),
                  pl.BlockSpec((1, D_MODEL), lambda i, j: (0, 0)),
                  pl.BlockSpec((D_MODEL, tn), lambda i, j: (0, j))],
        out_specs=[pl.BlockSpec((tm, tn), lambda i, j: (i, jnp.minimum(j, nf - 1))),
                   pl.BlockSpec((tm, tn), lambda i, j: (i, jnp.maximum(j - nf, 0)))],
        out_shape=[jax.ShapeDtypeStruct((t, ZF_WIDTH), F32),
                   jax.ShapeDtypeStruct((t, ZB_WIDTH), BF16)],
        scratch_shapes=[pltpu.VMEM((tm, D_MODEL), BF16)],
        compiler_params=_cparams(("parallel", "arbitrary")),
        name="inproj",
    )(x2, g, w)


def _dsa_prep_kernel(cq_ref, ckv_ref, sm_ref, qn_ref, kvn_ref, kig_ref, wuq_t_ref, wuqi_t_ref,
                     wuk_t_ref, qlat_ref, qidx_ref, widx_ref, kidx_ref, ckvn_ref, ckvt_ref):
    tm = cq_ref.shape[0]
    c_qt = _rms(cq_ref[...], qn_ref[0:1, :]).T.astype(BF16)
    q_t = _dot(wuq_t_ref[...], c_qt).astype(BF16)
    for h in range(A_HEADS):
        ql_t = (_dot(wuk_t_ref[h], q_t[h * A_HEAD_DIM:(h + 1) * A_HEAD_DIM, :])
                * (A_HEAD_DIM ** -0.5)).astype(BF16)
        qi_t = _dot(wuqi_t_ref[h], c_qt).astype(BF16)
        for j in range(tm // Q_BLOCK):
            cols = slice(j * Q_BLOCK, (j + 1) * Q_BLOCK)
            qlat_ref[j, :, h * Q_BLOCK:(h + 1) * Q_BLOCK] = ql_t[:, cols]
            qidx_ref[j, :, h * Q_BLOCK:(h + 1) * Q_BLOCK] = qi_t[:, cols]
    sm = sm_ref[...]
    w_t = sm.T[SM_W:SM_W + IDX_HEADS, :] * (IDX_HEADS ** -0.5 * IDX_DIM ** -0.5)
    for j in range(tm // Q_BLOCK):
        widx_ref[j] = w_t[:, j * Q_BLOCK:(j + 1) * Q_BLOCK]
    lane = lax.broadcasted_iota(I32, sm.shape, 1)
    inside = lane < IDX_DIM
    mu = jnp.sum(jnp.where(inside, sm, 0.0), axis=-1, keepdims=True) * (1.0 / IDX_DIM)
    xc = jnp.where(inside, sm - mu, 0.0)
    var = jnp.sum(xc * xc, axis=-1, keepdims=True) * (1.0 / IDX_DIM)
    kidx_ref[...] = (xc * lax.rsqrt(var + EPS) * kig_ref[0:1, :]).astype(BF16)
    ckvn = _rms(ckv_ref[...], kvn_ref[0:1, :])
    ckvn_ref[...] = ckvn.astype(BF16)
    ckvn_t = ckvn.T.astype(BF16)
    for j in range(tm // KEY_TILE):
        ckvt_ref[j] = ckvn_t[:, j * KEY_TILE:(j + 1) * KEY_TILE]


def _dsa_prep(z, qn, kvn, kig, wuq_t, wuqi_t, wuk_t, tm):
    t = z.shape[0]
    hq = A_HEADS * Q_BLOCK
    nqb = tm // Q_BLOCK
    qspec = pl.BlockSpec((nqb, LANES, hq), lambda i: (i, 0, 0))
    return pl.pallas_call(
        _dsa_prep_kernel,
        grid=(t // tm,),
        in_specs=[pl.BlockSpec((tm, Q_RANK), lambda i: (i, F_ACQ // Q_RANK)),
                  pl.BlockSpec((tm, KV_RANK), lambda i: (i, F_ACKV // KV_RANK)),
                  pl.BlockSpec((tm, LANES), lambda i: (i, F_SMALL // LANES)),
                  _resident((SUBLANES, Q_RANK)),
                  _resident((SUBLANES, KV_RANK)),
                  _resident((SUBLANES, LANES)),
                  _resident((A_WIDTH, Q_RANK)),
                  _resident((IDX_HEADS, LANES, Q_RANK)),
                  _resident((A_HEADS, KV_RANK, A_HEAD_DIM))],
        out_specs=[qspec, qspec,
                   pl.BlockSpec((nqb, IDX_HEADS, Q_BLOCK), lambda i: (i, 0, 0)),
                   pl.BlockSpec((tm, LANES), lambda i: (i, 0)),
                   pl.BlockSpec((tm, KV_RANK), lambda i: (i, 0)),
                   pl.BlockSpec((tm // KEY_TILE, KV_RANK, KEY_TILE), lambda i: (i, 0, 0))],
        out_shape=[jax.ShapeDtypeStruct((t // Q_BLOCK, KV_RANK, hq), BF16),
                   jax.ShapeDtypeStruct((t // Q_BLOCK, LANES, hq), BF16),
                   jax.ShapeDtypeStruct((t // Q_BLOCK, IDX_HEADS, Q_BLOCK), F32),
                   jax.ShapeDtypeStruct((t, LANES), BF16),
                   jax.ShapeDtypeStruct((t, KV_RANK), BF16),
                   jax.ShapeDtypeStruct((t // KEY_TILE, KV_RANK, KEY_TILE), BF16)],
        compiler_params=_cparams(("parallel",)),
        name="dsa_prep",
    )(z, z, z, qn, kvn, kig, wuq_t, wuqi_t, wuk_t)


def _fold(x, op, rows=SUBLANES):
    parts = [x[i:i + rows] for i in range(0, x.shape[0], rows)]
    while len(parts) > 1:
        parts = [op(parts[i], parts[i + 1]) for i in range(0, len(parts), 2)]
    return parts[0]


def _tile_loop(n, body, init, quads=False):
    def pair(j, carry):
        return body(j + 1, body(j, carry))

    def quad(j, carry):
        return pair(4 * j + 2, pair(4 * j, carry))

    start = 0
    carry = init
    if quads:
        carry = lax.fori_loop(0, n // 4, quad, carry)
        start = 4 * (n // 4)
    npair = (n - start) // 2
    carry = lax.fori_loop(0, npair, lambda j, c: pair(start + 2 * j, c), carry)
    return lax.fori_loop(start + 2 * npair, n, body, carry)


def _dsa_kernel(qidx_ref, qlat_ref, widx_ref, kidx_ref, ckv_ref, ckvt_ref, bias_ref, wuv_t_ref,
                out_ref, keys_scr, lg_scr, acc_scr, *, top_k):
    QB = Q_BLOCK
    KT = KEY_TILE
    qblk = pl.program_id(1)
    nk = (qblk * QB + QB + KT - 1) // KT
    kf = float(top_k)
    hcols = [slice(h * QB, (h + 1) * QB) for h in range(A_HEADS)]
    hrows = [slice(h * SUBLANES, (h + 1) * SUBLANES) for h in range(A_HEADS)]

    q_pos = qblk * QB + lax.broadcasted_iota(I32, (KT, QB), 1)
    key_off = lax.broadcasted_iota(I32, (KT, QB), 0)

    qi_t = qidx_ref[0]
    w_rows = widx_ref[0]

    def score_body(kt, carry):
        s_t = _dot(kidx_ref[0, kt], qi_t)
        tot = jnp.zeros((KT, QB), F32)
        for g in range(IDX_HEADS):
            tot = tot + w_rows[g:g + 1, :] * jnp.maximum(s_t[:, hcols[g]], 0.0)
        key_pos = kt * KT + key_off
        tot = jnp.where(key_pos <= q_pos, tot + 0.0, -jnp.inf)
        bits = pltpu.bitcast(tot, I32)
        keys_scr[kt] = bits ^ ((bits >> 31) & 0x7FFFFFFF)
        return carry

    _tile_loop(nk, score_body, 0, quads=True)

    def count(pred):
        def body(kt, acc):
            return acc + _fold(jnp.where(pred(keys_scr[kt], kt), 1.0, 0.0), jnp.add)
        acc = _tile_loop(nk, body, jnp.zeros((SUBLANES, QB), F32))
        return jnp.sum(acc, axis=0, keepdims=True)

    def count_ge(cand):
        return count(lambda keys, kt: keys >= cand)

    n_bits = jnp.where((qblk + 1) * QB <= top_k, 0, 31)
    c0 = count_ge(jnp.zeros((1, QB), I32))
    pos = (c0 >= kf) & (n_bits > 0)
    thr = jnp.where(pos, 0, INT_MIN).astype(I32)
    cnt = jnp.where(pos, c0, (nk * KT).astype(F32))

    def bit_body(i, carry):
        thr, cnt = carry
        cand = thr | (jnp.int32(1) << (30 - i))
        c = count_ge(cand)
        ok = c >= kf
        return jnp.where(ok, cand, thr), jnp.where(ok, c, cnt)

    thr, cnt = lax.fori_loop(0, n_bits, bit_body, (thr, cnt))

    over = jnp.max(cnt)

    @pl.when(over > kf)
    def _():
        need = kf - count(lambda keys, kt: keys > thr)
        r_k = lax.broadcasted_iota(I32, (KT, KT), 0)
        c_k = lax.broadcasted_iota(I32, (KT, KT), 1)
        lower = jnp.where(r_k >= c_k, 1.0, 0.0).astype(BF16)

        def demote(kt, before):
            keys = keys_scr[kt]
            tie = keys == thr
            rank = _dot(lower, jnp.where(tie, 1.0, 0.0).astype(BF16)) + before
            keys_scr[kt] = jnp.where(tie & (rank > need), thr - 1, keys)
            return rank[KT - 1:KT, :]

        _tile_loop(nk, demote, jnp.zeros((1, QB), F32))

    ql_t = qlat_ref[0]

    def pass_a(kt, mx):
        lg_t = _dot(ckv_ref[0, kt], ql_t)
        keys = keys_scr[kt]
        key_pos = kt * KT + key_off
        sel = (keys >= thr) & (key_pos <= q_pos)
        didx = jnp.minimum((qblk * QB - kt * KT) // QB, N_BIAS_TABLES - 1)
        new = []
        for h in range(A_HEADS):
            lh = jnp.where(sel, lg_t[:, hcols[h]] + bias_ref[didx, h], -jnp.inf)
            lg_scr[kt, :, hcols[h]] = lh
            new.append(jnp.maximum(mx[hrows[h]], _fold(lh, jnp.maximum)))
        return jnp.concatenate(new, axis=0)

    mx = _tile_loop(nk, pass_a, jnp.full((A_HEADS * SUBLANES, QB), -jnp.inf, F32), quads=True)
    m_rows = [jnp.max(mx[hrows[h]], axis=0, keepdims=True) for h in range(A_HEADS)]

    acc_scr[...] = jnp.zeros_like(acc_scr)

    def pass_b(kt, l):
        lg = lg_scr[kt]
        ps, new = [], []
        for h in range(A_HEADS):
            p = jnp.exp(lg[:, hcols[h]] - m_rows[h])
            new.append(l[hrows[h]] + _fold(p, jnp.add))
            ps.append(p.astype(BF16))
        acc_scr[...] = acc_scr[...] + _dot(ckvt_ref[0, kt], jnp.concatenate(ps, axis=1))
        return jnp.concatenate(new, axis=0)

    l = _tile_loop(nk, pass_b, jnp.zeros((A_HEADS * SUBLANES, QB), F32), quads=True)

    outs = []
    for h in range(A_HEADS):
        o_t = (acc_scr[:, hcols[h]] / jnp.sum(l[hrows[h]], axis=0, keepdims=True)).astype(BF16)
        outs.append(_dot(wuv_t_ref[h], o_t))
    out_ref[...] = jnp.concatenate(outs, axis=0).T.astype(BF16)


def _dsa(qidx_t, qlat_t, widx, kidx, ckv, ckv_t, bias_tab, wuv_t, b, s):
    nq = s // Q_BLOCK
    nt = s // KEY_TILE
    top_k = min(TOPK_MAX, s // 4)
    hq = A_HEADS * Q_BLOCK
    row = lambda bi, qi: bi * nq + qi
    qspec = pl.BlockSpec((1, LANES, hq), lambda bi, qi: (row(bi, qi), 0, 0))
    kspec = pl.BlockSpec((1, nt, KEY_TILE, LANES), lambda bi, qi: (bi, 0, 0, 0))
    return pl.pallas_call(
        functools.partial(_dsa_kernel, top_k=top_k),
        grid=(b, nq),
        in_specs=[qspec, qspec,
                  pl.BlockSpec((1, IDX_HEADS, Q_BLOCK), lambda bi, qi: (row(bi, qi), 0, 0)),
                  kspec, kspec,
                  pl.BlockSpec((1, nt, KV_RANK, KEY_TILE), lambda bi, qi: (bi, 0, 0, 0)),
                  _resident((N_BIAS_TABLES, A_HEADS, KEY_TILE, Q_BLOCK)),
                  _resident((A_HEADS, A_HEAD_DIM, KV_RANK))],
        out_specs=pl.BlockSpec((Q_BLOCK, A_WIDTH), lambda bi, qi: (row(bi, qi), 0)),
        out_shape=jax.ShapeDtypeStruct((b * s, A_WIDTH), BF16),
        scratch_shapes=[pltpu.VMEM((nt, KEY_TILE, Q_BLOCK), I32),
                        pltpu.VMEM((nt, KEY_TILE, hq), F32),
                        pltpu.VMEM((KV_RANK, hq), F32)],
        compiler_params=_cparams(("parallel", "arbitrary")),
        name="dsa",
    )(qidx_t, qlat_t, widx, kidx.reshape(b, nt, KEY_TILE, LANES), ckv.reshape(b, nt, KEY_TILE, LANES),
      ckv_t.reshape(b, nt, KV_RANK, KEY_TILE), bias_tab, wuv_t)


FF_CHUNK = 256
TAIL_TILE = 512


def _mlstm_chunk(qk, sm, v_ref, o_ref, rows, gbias, ng_ref, consts, state):
    L = CHUNK
    causal, lower, upper, is_f, ones = consts
    c_st, m_st = state
    hs = range(M_HEADS)
    exact = dict(preferred_element_type=F32, precision=lax.Precision.HIGHEST)
    kt_all = qk[:, M_HEADS * M_QK:].T
    g = sm + gbias
    xg = jnp.where(is_f, jax.nn.log_sigmoid(g), g)
    rows_if = xg.T[SM_I:SM_I + 2 * M_HEADS, :]
    cum_r = jnp.dot(rows_if, upper, **exact)
    bc = jnp.dot(lower, xg, **exact)
    bc_q = [jnp.broadcast_to(bc[:, SM_F + h:SM_F + h + 1], (L, LANES)) for h in hs]
    b_r = [rows_if[h:h + 1, :] - cum_r[M_HEADS + h:M_HEADS + h + 1, :] for h in hs]
    b_last = [bc_q[h][L - 1:L, :] for h in hs]
    d_log = [jnp.where(causal, bc_q[h] + b_r[h], -jnp.inf) for h in hs]
    inter = [bc_q[h] + m_st[h] for h in hs]
    m_row = [jnp.maximum(inter[h], jnp.max(d_log[h], axis=-1, keepdims=True)) for h in hs]
    w_intra = [jnp.exp(d_log[h] - m_row[h]) for h in hs]
    w_inter = [jnp.exp(inter[h] - m_row[h]) for h in hs]
    floor = [jnp.exp(-m_row[h]) for h in hs]

    qb = [(qk[:, h * M_QK:(h + 1) * M_QK] * (M_QK ** -0.5)).astype(BF16) for h in hs]
    kt = [kt_all[h * M_QK:(h + 1) * M_QK, :] for h in hs]
    vaug = [jnp.concatenate([v_ref[rows, h * M_V:(h + 1) * M_V], ones], axis=1) for h in hs]
    sc = [_dot(qb[h], kt[h].astype(BF16)) for h in hs]
    cross = [_dot(qb[h], c_st[h].astype(BF16)) for h in hs]

    g_r = [b_last[h] + b_r[h] for h in hs]
    m_new = [jnp.maximum(b_last[h] + m_st[h], jnp.max(g_r[h], axis=-1, keepdims=True)) for h in hs]
    wk_r = [jnp.exp(g_r[h] - m_new[h]) for h in hs]
    decay = [jnp.exp(b_last[h] + m_st[h] - m_new[h]) for h in hs]
    upd = [_dot((kt[h] * wk_r[h]).astype(BF16), vaug[h]) for h in hs]

    sc = [(sc[h] * w_intra[h]).astype(BF16) for h in hs]
    num = [_dot(sc[h], vaug[h]) + jnp.concatenate([w_inter[h], w_inter[h]], axis=1) * cross[h] for h in hs]
    out = [num[h][:, :M_V] / jnp.maximum(jnp.abs(num[h][:, M_V:]), floor[h]) for h in hs]
    c_new = [jnp.concatenate([decay[h], decay[h]], axis=1) * c_st[h] + upd[h] for h in hs]
    y = [_group_norm(out[h], ng_ref[0:1, h * M_V:(h + 1) * M_V]) for h in hs]
    hm = [(y[h] * jax.nn.sigmoid(o_ref[rows, h * M_V:(h + 1) * M_V].astype(F32))).astype(BF16) for h in hs]
    return jnp.concatenate(hm, axis=1), (c_new, m_new)


def _ret_chunk(q_ref, k_ref, v_ref, g_ref, rows, cosf, sinf, intra_ref, qdec_ref, kdec_ref, cdec_ref,
               ng_ref, state):
    hs = range(R_HEADS)
    sl = [slice(h * R_QK, (h + 1) * R_QK) for h in hs]
    half = R_QK // 2

    def rot(x):
        return x * cosf + pltpu.roll(x, half, 1) * sinf

    qr = [rot(q_ref[rows, sl[h]]).astype(BF16) for h in hs]
    kt = [(rot(k_ref[rows, sl[h]]) * (R_QK ** -0.5)).T for h in hs]
    vb = [v_ref[rows, sl[h]] for h in hs]
    sc = [_dot(qr[h], kt[h].astype(BF16)) for h in hs]
    cross = [_dot(qr[h], state[h].astype(BF16)) for h in hs]
    upd = [_dot((kt[h] * kdec_ref[h:h + 1, :]).astype(BF16), vb[h]) for h in hs]
    sc = [(sc[h] * intra_ref[h]).astype(BF16) for h in hs]
    out = [_dot(sc[h], vb[h]) + qdec_ref[h] * cross[h] for h in hs]
    new = [cdec_ref[h:h + 1, :] * state[h] + upd[h] for h in hs]
    y = [_group_norm(out[h], ng_ref[0:1, sl[h]]) for h in hs]
    hr = []
    for h in hs:
        gt = g_ref[rows, sl[h]].astype(F32)
        hr.append((y[h] * (gt * jax.nn.sigmoid(gt))).astype(BF16))
    return jnp.concatenate(hr, axis=1), new


def _tail_kernel(x_ref, ha_ref, gm_ref, ga_ref, gr_ref,
                 qk_ref, sm_ref, mv_ref, mo_ref, rq_ref, rk_ref, rv_ref, rg_ref, cos_ref, sin_ref,
                 conv_ref, gb_ref, mng_ref, intra_ref, qdec_ref, kdec_ref, cdec_ref, rng_ref,
                 pm_ref, pa_ref, pr_ref, wo_ref, n2_ref, wg_ref, wu_ref, wd_ref, fg_ref,
                 out_ref, hm_scr, hr_scr, xbuf, cst, mst, rst, *, final, n_tiles, tiles_per_row):
    L = CHUNK
    TM = TAIL_TILE
    i = pl.program_id(0)
    scan_tile = jnp.minimum(i, n_tiles - 1)
    cur = lax.rem(i, 2)
    prev = lax.rem(i + 1, 2)

    @pl.when(i == 0)
    def _():
        hm_scr[...] = jnp.zeros_like(hm_scr)
        hr_scr[...] = jnp.zeros_like(hr_scr)

    @pl.when(lax.rem(scan_tile, tiles_per_row) == 0)
    def _():
        xbuf[0:SUBLANES, :] = jnp.zeros((SUBLANES, 2 * M_HEADS * M_QK), F32)
        cst[...] = jnp.zeros_like(cst)
        mst[...] = jnp.zeros_like(mst)
        rst[...] = jnp.zeros_like(rst)

    y = (jax.nn.sigmoid(gm_ref[...].astype(F32)) * _dot(hm_scr[prev], pm_ref[...])
         + jax.nn.sigmoid(ga_ref[...].astype(F32)) * _dot(ha_ref[...], pa_ref[...])
         + jax.nn.sigmoid(gr_ref[...].astype(F32)) * _dot(hr_scr[prev], pr_ref[...]))
    x1 = x_ref[...] + _dot(y.astype(BF16), wo_ref[...])
    h2 = _rms(x1, n2_ref[0:1, :]).astype(BF16)

    x_qk = qk_ref[...]
    xbuf[SUBLANES:SUBLANES + TM, :] = x_qk
    w = conv_ref[...]
    conv = w[CONV_W - 1:CONV_W, :] * x_qk
    for j in range(CONV_W - 1):
        off = SUBLANES - (CONV_W - 1) + j
        conv = conv + w[j:j + 1, :] * xbuf[off:off + TM, :]
    xbuf[0:SUBLANES, :] = x_qk[TM - SUBLANES:TM, :]
    qk_all = conv * jax.nn.sigmoid(conv)

    r_i = lax.broadcasted_iota(I32, (L, L), 0)
    c_i = lax.broadcasted_iota(I32, (L, L), 1)
    lane = lax.broadcasted_iota(I32, (L, LANES), 1)
    causal = r_i >= c_i
    consts = (causal, causal.astype(F32), (r_i <= c_i).astype(F32),
              (lane >= SM_F) & (lane < SM_F + M_HEADS), jnp.ones((L, M_V), BF16))
    gbias = gb_ref[0:1, :]
    m_state = ([cst[h] for h in range(M_HEADS)], [mst[h:h + 1, :] for h in range(M_HEADS)])
    r_state = [rst[h] for h in range(R_HEADS)]

    def scan_unit(u, m_state, r_state):
        c = u // 2
        rows = slice(c * L, (c + 1) * L)
        if u % 2 == 0:
            hm, m_state = _mlstm_chunk(qk_all[rows], sm_ref[rows, :], mv_ref, mo_ref, rows, gbias,
                                       mng_ref, consts, m_state)
            hm_scr[cur, rows, :] = hm
        else:
            hr, r_state = _ret_chunk(rq_ref, rk_ref, rv_ref, rg_ref, rows, cos_ref[rows, :],
                                     sin_ref[rows, :], intra_ref, qdec_ref, kdec_ref, cdec_ref,
                                     rng_ref, r_state)
            hr_scr[cur, rows, :] = hr
        return m_state, r_state

    n_units = 2 * (TM // L)
    n_ff = D_FF // FF_CHUNK
    assert n_units <= n_ff
    acc = x1
    for c in range(n_ff):
        cs = slice(c * FF_CHUNK, (c + 1) * FF_CHUNK)
        gate = _dot(h2, wg_ref[:, cs])
        up = _dot(h2, wu_ref[:, cs])
        act = (gate * jax.nn.sigmoid(gate) * up).astype(BF16)
        acc = acc + _dot(act, wd_ref[cs, :])
        if c < n_units:
            m_state, r_state = scan_unit(c, m_state, r_state)

    for h in range(M_HEADS):
        cst[h] = m_state[0][h]
    mst[0:M_HEADS, :] = jnp.concatenate(m_state[1], axis=0)
    for h in range(R_HEADS):
        rst[h] = r_state[h]
    if final:
        acc = _rms(acc, fg_ref[0:1, :])
    out_ref[...] = acc


def _tail(x2, ha, zf, zb, cosf, sinf, conv_w, gbias, mng, intra, qdec, kdec, cdec, rng,
          pm, pa, pr, wo, n2, wg, wu, wd, fg, s, final):
    t = x2.shape[0]
    tm = TAIL_TILE
    n_tiles = t // tm
    tiles_per_row = s // tm
    w512 = 2 * M_HEADS * M_QK
    merge_tile = lambda i: jnp.maximum(i - 1, 0)
    scan_tile = lambda i: jnp.minimum(i, n_tiles - 1)
    mspec = lambda w, cb=0: pl.BlockSpec((tm, w), lambda i: (merge_tile(i), cb))
    sspec = lambda w, off: pl.BlockSpec((tm, w), lambda i: (scan_tile(i), off // w))
    pspec = pl.BlockSpec((tm, R_QK), lambda i: (lax.rem(scan_tile(i), tiles_per_row), 0))
    return pl.pallas_call(
        functools.partial(_tail_kernel, final=final, n_tiles=n_tiles, tiles_per_row=tiles_per_row),
        grid=(n_tiles + 1,),
        in_specs=[mspec(D_MODEL), mspec(A_WIDTH),
                  mspec(D_MODEL, B_GM // D_MODEL), mspec(D_MODEL, B_GA // D_MODEL),
                  mspec(D_MODEL, B_GR // D_MODEL),
                  sspec(w512, F_MQK), sspec(LANES, F_SMALL), sspec(M_WIDTH, B_MV), sspec(M_WIDTH, B_MO),
                  sspec(R_WIDTH, F_RQ), sspec(R_WIDTH, F_RK), sspec(R_WIDTH, B_RV), sspec(R_WIDTH, B_RG),
                  pspec, pspec,
                  _resident((SUBLANES, w512)), _resident((SUBLANES, LANES)), _resident((SUBLANES, M_WIDTH)),
                  _resident((R_HEADS, CHUNK, CHUNK)), _resident((R_HEADS, CHUNK, R_V)),
                  _resident((SUBLANES, CHUNK)), _resident((SUBLANES, R_V)), _resident((SUBLANES, R_WIDTH)),
                  _resident((M_WIDTH, D_MODEL)), _resident((A_WIDTH, D_MODEL)),
                  _resident((R_WIDTH, D_MODEL)), _resident((D_MODEL, D_MODEL)),
                  _resident((SUBLANES, D_MODEL)),
                  _resident((D_MODEL, D_FF)), _resident((D_MODEL, D_FF)),
                  _resident((D_FF, D_MODEL)), _resident((SUBLANES, D_MODEL))],
        out_specs=mspec(D_MODEL),
        out_shape=jax.ShapeDtypeStruct((t, D_MODEL), F32),
        scratch_shapes=[pltpu.VMEM((2, tm, M_WIDTH), BF16),
                        pltpu.VMEM((2, tm, R_WIDTH), BF16),
                        pltpu.VMEM((SUBLANES + tm, w512), F32),
                        pltpu.VMEM((M_HEADS, M_QK, 2 * M_V), F32),
                        pltpu.VMEM((SUBLANES, LANES), F32),
                        pltpu.VMEM((R_HEADS, R_QK, R_V), F32)],
        compiler_params=_cparams(("arbitrary",), VMEM_LIMIT_TAIL),
        name="tail",
    )(x2, ha, zb, zb, zb, zf, zf, zb, zb, zf, zf, zb, zb, cosf, sinf,
      conv_w, gbias, mng, intra, qdec, kdec, cdec, rng, pm, pa, pr, wo, n2, wg, wu, wd, fg)


def _pad_rows(v, rows=SUBLANES):
    v = jnp.atleast_2d(v).astype(F32)
    return jnp.pad(v, ((0, rows - v.shape[0]), (0, 0)))


def _t5_bucket(dist):
    max_exact = N_BUCKETS // 2
    d_f = jnp.maximum(dist, 1).astype(F32)
    large = max_exact + (jnp.log(d_f / max_exact) / math.log(MAX_DISTANCE / max_exact)
                         * (N_BUCKETS - max_exact)).astype(I32)
    large = jnp.minimum(large, N_BUCKETS - 1)
    return jnp.where(dist < max_exact, dist, large)


def _bias_table(rel_bias):
    d = jnp.arange(N_BIAS_TABLES, dtype=I32)[:, None, None]
    j = jnp.arange(KEY_TILE, dtype=I32)[None, :, None]
    i = jnp.arange(Q_BLOCK, dtype=I32)[None, None, :]
    bucket = _t5_bucket(jnp.maximum(i + d * Q_BLOCK - j, 0))
    onehot = (bucket[None] == jnp.arange(N_BUCKETS, dtype=I32)[:, None, None, None]).astype(F32)
    return jnp.einsum("nh,ndkq->dhkq", rel_bias.astype(F32), onehot, precision=lax.Precision.HIGHEST)


def _rotary_tables(s):
    half = R_QK // 2
    freqs = ROPE_BASE ** (-jnp.linspace(0.0, 1.0, half, dtype=F32))
    ang = jnp.arange(s, dtype=F32)[:, None] * freqs[None, :]
    cos, sin = jnp.cos(ang), jnp.sin(ang)
    return jnp.concatenate([cos, cos], axis=1), jnp.concatenate([-sin, sin], axis=1)


def _retention_tables():
    h = R_HEADS
    log_gamma = jnp.log1p(-jnp.exp2(-5.0 - jnp.arange(h, dtype=F32)))
    pos = jnp.arange(CHUNK, dtype=F32)
    diff = pos[:, None] - pos[None, :]
    intra = jnp.where(diff >= 0, jnp.exp(jnp.maximum(diff, 0.0)[None] * log_gamma[:, None, None]), 0.0)
    q_decay = jnp.exp((pos[None, :] + 1.0) * log_gamma[:, None])
    k_decay = jnp.exp((CHUNK - 1.0 - pos[None, :]) * log_gamma[:, None])
    chunk_decay = jnp.exp(CHUNK * log_gamma)
    qdec = jnp.broadcast_to(q_decay[:, :, None], (h, CHUNK, R_V))
    kdec = _pad_rows(k_decay)
    cdec = _pad_rows(jnp.broadcast_to(chunk_decay[:, None], (h, R_V)))
    return intra, qdec, kdec, cdec


def _layout_w_in(w_in):
    parts = jnp.split(w_in, [int(c) for c in np.cumsum(IN_SPLITS)[:-1]], axis=-1)
    (m_q, m_k, m_v, m_i, m_f, m_o, a_cq, a_kidx, a_widx, a_ckv,
     r_q, r_k, r_v, r_g, g_m, g_a, g_r) = parts
    small = jnp.concatenate([a_kidx, m_i, m_f, a_widx], axis=-1)
    small = jnp.pad(small, ((0, 0), (0, LANES - small.shape[-1])))
    w = jnp.concatenate([m_q, m_k, a_cq, a_ckv, small, r_q, r_k,
                         m_v, m_o, r_v, r_g, g_m, g_a, g_r], axis=-1)
    assert w.shape[-1] == ZF_WIDTH + ZB_WIDTH
    return w.astype(BF16)


def _row_tile(t, want):
    while t % want:
        want //= 2
    return want


def kernel(x, rel_bias, final_norm_g, norm1_g, w_in, m_conv, m_ibias, m_fbias, m_norm_g, a_qnorm_g, a_wuq, a_wuq_idx, a_kidx_g, a_kvnorm_g, a_wuk, a_wuv, r_norm_g, p_m, p_a, p_r, w_out, norm2_g, w_gate, w_up, w_down):
    b, s, d = x.shape
    depth = w_in.shape[0]
    assert d == D_MODEL and s % KEY_TILE == 0 and s % TAIL_TILE == 0
    assert SM_F == SM_I + M_HEADS and M_V == LANES and R_V == LANES and CHUNK == LANES
    t = b * s
    tm_in = _row_tile(t, 2048)
    tm = _row_tile(t, 512)

    bias_tab = _bias_table(rel_bias)
    cosf, sinf = _rotary_tables(s)
    intra, qdec, kdec, cdec = _retention_tables()
    fg = _pad_rows(final_norm_g)

    x2 = x.reshape(t, d)
    for l in range(depth):
        zf, zb = _inproj(x2, norm1_g[l][None, :], _layout_w_in(w_in[l]), tm_in, 1024)

        gbias = jnp.zeros((LANES,), F32)
        gbias = gbias.at[SM_I:SM_I + M_HEADS].set(m_ibias[l]).at[SM_F:SM_F + M_HEADS].set(m_fbias[l])

        wuqi_t = a_wuq_idx[l].reshape(Q_RANK, IDX_HEADS, IDX_DIM).transpose(1, 2, 0)
        wuqi_t = jnp.pad(wuqi_t, ((0, 0), (0, LANES - IDX_DIM), (0, 0))).astype(BF16)
        kig = jnp.pad(a_kidx_g[l], (0, LANES - IDX_DIM))
        qlat_t, qidx_t, widx, kidx, ckv, ckv_t = _dsa_prep(
            zf, _pad_rows(a_qnorm_g[l]), _pad_rows(a_kvnorm_g[l]), _pad_rows(kig),
            a_wuq[l].T.astype(BF16), wuqi_t, jnp.swapaxes(a_wuk[l], 1, 2).astype(BF16), tm)
        ha = _dsa(qidx_t, qlat_t, widx, kidx, ckv, ckv_t, bias_tab,
                  jnp.swapaxes(a_wuv[l], 1, 2).astype(BF16), b, s)

        x2 = _tail(x2, ha, zf, zb, cosf, sinf, _pad_rows(m_conv[l]), _pad_rows(gbias),
                   _pad_rows(m_norm_g[l]), intra, qdec, kdec, cdec, _pad_rows(r_norm_g[l]),
                   p_m[l].astype(BF16), p_a[l].astype(BF16), p_r[l].astype(BF16),
                   w_out[l].astype(BF16), _pad_rows(norm2_g[l]), w_gate[l].astype(BF16),
                   w_up[l].astype(BF16), w_down[l].astype(BF16), fg, s, final=(l == depth - 1))
    return x2.reshape(b, s, d)
```

```python
import functools
import math

import numpy as np
import jax
import jax.numpy as jnp
from jax import lax
from jax.experimental import pallas as pl
from jax.experimental.pallas import tpu as pltpu

F32 = jnp.float32
BF16 = jnp.bfloat16
I32 = jnp.int32

D_MODEL = 1024
M_HEADS, M_QK, M_V = 4, 64, 128
M_WIDTH = M_HEADS * M_V
CONV_W = 4
A_HEADS, A_HEAD_DIM = 8, 64
A_WIDTH = A_HEADS * A_HEAD_DIM
Q_RANK, KV_RANK = 256, 128
IDX_HEADS, IDX_DIM = 8, 64
TOPK_MAX = 256
R_HEADS, R_QK, R_V = 4, 128, 128
R_WIDTH = R_HEADS * R_V
CHUNK = 128
Q_BLOCK = 256
N_BUCKETS = 32
MAX_DISTANCE = 128
ROPE_BASE = 10000.0
EPS = 1e-6
D_FF = -(((-8 * D_MODEL) // 3) // 256) * 256
IN_SPLITS = (M_HEADS * M_QK, M_HEADS * M_QK, M_WIDTH, M_HEADS, M_HEADS, M_WIDTH,
             Q_RANK, IDX_DIM, IDX_HEADS, KV_RANK,
             R_HEADS * R_QK, R_HEADS * R_QK, R_WIDTH, R_WIDTH,
             D_MODEL, D_MODEL, D_MODEL)

LANES = 128
SUBLANES = 8
VMEM_LIMIT = 56 * 1024 * 1024
VMEM_LIMIT_TAIL = 62 * 1024 * 1024

F_MQK = 0
F_ACQ = 512
F_ACKV = 768
F_SMALL = 896
F_RQ, F_RK = 1024, 1536
ZF_WIDTH = 2048
B_MV, B_MO, B_RV, B_RG = 0, 512, 1024, 1536
B_GM, B_GA, B_GR = 2048, 3072, 4096
ZB_WIDTH = 5120
SM_KIDX, SM_I, SM_F, SM_W = 0, 64, 68, 72

KEY_TILE = 256
N_BIAS_TABLES = -(-(MAX_DISTANCE + KEY_TILE - 1) // Q_BLOCK) + 1
INT_MIN = -2 ** 31


def _cparams(sem, vmem_limit=VMEM_LIMIT):
    return pltpu.CompilerParams(dimension_semantics=sem, vmem_limit_bytes=vmem_limit)


def _resident(shape):
    nd = len(shape)
    return pl.BlockSpec(shape, lambda *_: (0,) * nd, pipeline_mode=pl.Buffered(1))


def _rms(x, g):
    return x * lax.rsqrt(jnp.mean(x * x, axis=-1, keepdims=True) + EPS) * g


def _group_norm(x, g):
    mu = jnp.mean(x, axis=-1, keepdims=True)
    xc = x - mu
    return xc * lax.rsqrt(jnp.mean(xc * xc, axis=-1, keepdims=True) + EPS) * g


def _dot(a, b):
    return jnp.dot(a, b, preferred_element_type=F32)


def _inproj_kernel(x_ref, g_ref, w_ref, zf_ref, zb_ref, h_scr, *, nf):
    j = pl.program_id(1)

    @pl.when(j == 0)
    def _():
        h_scr[...] = _rms(x_ref[...], g_ref[...]).astype(BF16)

    @pl.when(j < nf)
    def _():
        zf_ref[...] = _dot(h_scr[...], w_ref[...])

    @pl.when(j >= nf)
    def _():
        zb_ref[...] = _dot(h_scr[...], w_ref[...]).astype(BF16)


def _inproj(x2, g, w, tm, tn):
    t = x2.shape[0]
    nf = ZF_WIDTH // tn
    return pl.pallas_call(
        functools.partial(_inproj_kernel, nf=nf),
        grid=(t // tm, (ZF_WIDTH + ZB_WIDTH) // tn),
        in_specs=[pl.BlockSpec((tm, D_MODEL), lambda i, j: (i, 0)),
                  pl.BlockSpec((1, D_MODEL), lambda i, j: (0, 0)),
                  pl.BlockSpec((D_MODEL, tn), lambda i, j: (0, j))],
        out_specs=[pl.BlockSpec((tm, tn), lambda i, j: (i, jnp.minimum(j, nf - 1))),
                   pl.BlockSpec((tm, tn), lambda i, j: (i, jnp.maximum(j - nf, 0)))],
        out_shape=[jax.ShapeDtypeStruct((t, ZF_WIDTH), F32),
                   jax.ShapeDtypeStruct((t, ZB_WIDTH), BF16)],
        scratch_shapes=[pltpu.VMEM((tm, D_MODEL), BF16)],
        compiler_params=_cparams(("parallel", "arbitrary")),
        name="inproj",
    )(x2, g, w)


def _dsa_prep_kernel(cq_ref, ckv_ref, sm_ref, qn_ref, kvn_ref, kig_ref, wuq_t_ref, wuqi_t_ref,
                     wuk_t_ref, qlat_ref, qidx_ref, widx_ref, kidx_ref, ckvn_ref, ckvt_ref):
    tm = cq_ref.shape[0]
    c_qt = _rms(cq_ref[...], qn_ref[0:1, :]).T.astype(BF16)
    q_t = _dot(wuq_t_ref[...], c_qt).astype(BF16)
    for h in range(A_HEADS):
        ql_t = (_dot(wuk_t_ref[h], q_t[h * A_HEAD_DIM:(h + 1) * A_HEAD_DIM, :])
                * (A_HEAD_DIM ** -0.5)).astype(BF16)
        qi_t = _dot(wuqi_t_ref[h], c_qt).astype(BF16)
        for j in range(tm // Q_BLOCK):
            cols = slice(j * Q_BLOCK, (j + 1) * Q_BLOCK)
            qlat_ref[j, :, h * Q_BLOCK:(h + 1) * Q_BLOCK] = ql_t[:, cols]
            qidx_ref[j, :, h * Q_BLOCK:(h + 1) * Q_BLOCK] = qi_t[:, cols]
    sm = sm_ref[...]
    w_t = sm.T[SM_W:SM_W + IDX_HEADS, :] * (IDX_HEADS ** -0.5 * IDX_DIM ** -0.5)
    for j in range(tm // Q_BLOCK):
        widx_ref[j] = w_t[:, j * Q_BLOCK:(j + 1) * Q_BLOCK]
    lane = lax.broadcasted_iota(I32, sm.shape, 1)
    inside = lane < IDX_DIM
    mu = jnp.sum(jnp.where(inside, sm, 0.0), axis=-1, keepdims=True) * (1.0 / IDX_DIM)
    xc = jnp.where(inside, sm - mu, 0.0)
    var = jnp.sum(xc * xc, axis=-1, keepdims=True) * (1.0 / IDX_DIM)
    kidx_ref[...] = (xc * lax.rsqrt(var + EPS) * kig_ref[0:1, :]).astype(BF16)
    ckvn = _rms(ckv_ref[...], kvn_ref[0:1, :])
    ckvn_ref[...] = ckvn.astype(BF16)
    ckvn_t = ckvn.T.astype(BF16)
    for j in range(tm // KEY_TILE):
        ckvt_ref[j] = ckvn_t[:, j * KEY_TILE:(j + 1) * KEY_TILE]


def _dsa_prep(z, qn, kvn, kig, wuq_t, wuqi_t, wuk_t, tm):
    t = z.shape[0]
    hq = A_HEADS * Q_BLOCK
    nqb = tm // Q_BLOCK
    qspec = pl.BlockSpec((nqb, LANES, hq), lambda i: (i, 0, 0))
    return pl.pallas_call(
        _dsa_prep_kernel,
        grid=(t // tm,),
        in_specs=[pl.BlockSpec((tm, Q_RANK), lambda i: (i, F_ACQ // Q_RANK)),
                  pl.BlockSpec((tm, KV_RANK), lambda i: (i, F_ACKV // KV_RANK)),
                  pl.BlockSpec((tm, LANES), lambda i: (i, F_SMALL // LANES)),
                  _resident((SUBLANES, Q_RANK)),
                  _resident((SUBLANES, KV_RANK)),
                  _resident((SUBLANES, LANES)),
                  _resident((A_WIDTH, Q_RANK)),
                  _resident((IDX_HEADS, LANES, Q_RANK)),
                  _resident((A_HEADS, KV_RANK, A_HEAD_DIM))],
        out_specs=[qspec, qspec,
                   pl.BlockSpec((nqb, IDX_HEADS, Q_BLOCK), lambda i: (i, 0, 0)),
                   pl.BlockSpec((tm, LANES), lambda i: (i, 0)),
                   pl.BlockSpec((tm, KV_RANK), lambda i: (i, 0)),
                   pl.BlockSpec((tm // KEY_TILE, KV_RANK, KEY_TILE), lambda i: (i, 0, 0))],
        out_shape=[jax.ShapeDtypeStruct((t // Q_BLOCK, KV_RANK, hq), BF16),
                   jax.ShapeDtypeStruct((t // Q_BLOCK, LANES, hq), BF16),
                   jax.ShapeDtypeStruct((t // Q_BLOCK, IDX_HEADS, Q_BLOCK), F32),
                   jax.ShapeDtypeStruct((t, LANES), BF16),
                   jax.ShapeDtypeStruct((t, KV_RANK), BF16),
                   jax.ShapeDtypeStruct((t // KEY_TILE, KV_RANK, KEY_TILE), BF16)],
        compiler_params=_cparams(("parallel",)),
        name="dsa_prep",
    )(z, z, z, qn, kvn, kig, wuq_t, wuqi_t, wuk_t)


def _fold(x, op, rows=SUBLANES):
    parts = [x[i:i + rows] for i in range(0, x.shape[0], rows)]
    while len(parts) > 1:
        parts = [op(parts[i], parts[i + 1]) for i in range(0, len(parts), 2)]
    return parts[0]


def _tile_loop(n, body, init, quads=False):
    def pair(j, carry):
        return body(j + 1, body(j, carry))

    def quad(j, carry):
        return pair(4 * j + 2, pair(4 * j, carry))

    start = 0
    carry = init
    if quads:
        carry = lax.fori_loop(0, n // 4, quad, carry)
        start = 4 * (n // 4)
    npair = (n - start) // 2
    carry = lax.fori_loop(0, npair, lambda j, c: pair(start + 2 * j, c), carry)
    return lax.fori_loop(start + 2 * npair, n, body, carry)


def _dsa_kernel(qidx_ref, qlat_ref, widx_ref, kidx_ref, ckv_ref, ckvt_ref, bias_ref, wuv_t_ref,
                out_ref, keys_scr, lg_scr, acc_scr, *, top_k):
    QB = Q_BLOCK
    KT = KEY_TILE
    qblk = pl.program_id(1)
    nk = (qblk * QB + QB + KT - 1) // KT
    kf = float(top_k)
    hcols = [slice(h * QB, (h + 1) * QB) for h in range(A_HEADS)]
    hrows = [slice(h * SUBLANES, (h + 1) * SUBLANES) for h in range(A_HEADS)]

    q_pos = qblk * QB + lax.broadcasted_iota(I32, (KT, QB), 1)
    key_off = lax.broadcasted_iota(I32, (KT, QB), 0)

    qi_t = qidx_ref[0]
    w_rows = widx_ref[0]

    def score_body(kt, carry):
        s_t = _dot(kidx_ref[0, kt], qi_t)
        tot = jnp.zeros((KT, QB), F32)
        for g in range(IDX_HEADS):
            tot = tot + w_rows[g:g + 1, :] * jnp.maximum(s_t[:, hcols[g]], 0.0)
        key_pos = kt * KT + key_off
        tot = jnp.where(key_pos <= q_pos, tot + 0.0, -jnp.inf)
        bits = pltpu.bitcast(tot, I32)
        keys_scr[kt] = bits ^ ((bits >> 31) & 0x7FFFFFFF)
        return carry

    _tile_loop(nk, score_body, 0, quads=True)

    def count(pred):
        def body(kt, acc):
            return acc + _fold(jnp.where(pred(keys_scr[kt], kt), 1.0, 0.0), jnp.add)
        acc = _tile_loop(nk, body, jnp.zeros((SUBLANES, QB), F32))
        return jnp.sum(acc, axis=0, keepdims=True)

    def count_ge(cand):
        return count(lambda keys, kt: keys >= cand)

    n_bits = jnp.where((qblk + 1) * QB <= top_k, 0, 31)
    c0 = count_ge(jnp.zeros((1, QB), I32))
    pos = (c0 >= kf) & (n_bits > 0)
    thr = jnp.where(pos, 0, INT_MIN).astype(I32)
    cnt = jnp.where(pos, c0, (nk * KT).astype(F32))

    def bit_body(i, carry):
        thr, cnt = carry
        cand = thr | (jnp.int32(1) << (30 - i))
        c = count_ge(cand)
        ok = c >= kf
        return jnp.where(ok, cand, thr), jnp.where(ok, c, cnt)

    thr, cnt = lax.fori_loop(0, n_bits, bit_body, (thr, cnt))

    over = jnp.max(cnt)

    @pl.when(over > kf)
    def _():
        need = kf - count(lambda keys, kt: keys > thr)
        r_k = lax.broadcasted_iota(I32, (KT, KT), 0)
        c_k = lax.broadcasted_iota(I32, (KT, KT), 1)
        lower = jnp.where(r_k >= c_k, 1.0, 0.0).astype(BF16)

        def demote(kt, before):
            keys = keys_scr[kt]
            tie = keys == thr
            rank = _dot(lower, jnp.where(tie, 1.0, 0.0).astype(BF16)) + before
            keys_scr[kt] = jnp.where(tie & (rank > need), thr - 1, keys)
            return rank[KT - 1:KT, :]

        _tile_loop(nk, demote, jnp.zeros((1, QB), F32))

    ql_t = qlat_ref[0]

    def pass_a(kt, mx):
        lg_t = _dot(ckv_ref[0, kt], ql_t)
        keys = keys_scr[kt]
        key_pos = kt * KT + key_off
        sel = (keys >= thr) & (key_pos <= q_pos)
        didx = jnp.minimum((qblk * QB - kt * KT) // QB, N_BIAS_TABLES - 1)
        new = []
        for h in range(A_HEADS):
            lh = jnp.where(sel, lg_t[:, hcols[h]] + bias_ref[didx, h], -jnp.inf)
            lg_scr[kt, :, hcols[h]] = lh
            new.append(jnp.maximum(mx[hrows[h]], _fold(lh, jnp.maximum)))
        return jnp.concatenate(new, axis=0)

    mx = _tile_loop(nk, pass_a, jnp.full((A_HEADS * SUBLANES, QB), -jnp.inf, F32), quads=True)
    m_rows = [jnp.max(mx[hrows[h]], axis=0, keepdims=True) for h in range(A_HEADS)]

    acc_scr[...] = jnp.zeros_like(acc_scr)

    def pass_b(kt, l):
        lg = lg_scr[kt]
        ps, new = [], []
        for h in range(A_HEADS):
            p = jnp.exp(lg[:, hcols[h]] - m_rows[h])
            new.append(l[hrows[h]] + _fold(p, jnp.add))
            ps.append(p.astype(BF16))
        acc_scr[...] = acc_scr[...] + _dot(ckvt_ref[0, kt], jnp.concatenate(ps, axis=1))
        return jnp.concatenate(new, axis=0)

    l = _tile_loop(nk, pass_b, jnp.zeros((A_HEADS * SUBLANES, QB), F32), quads=True)

    outs = []
    for h in range(A_HEADS):
        o_t = (acc_scr[:, hcols[h]] / jnp.sum(l[hrows[h]], axis=0, keepdims=True)).astype(BF16)
        outs.append(_dot(wuv_t_ref[h], o_t))
    out_ref[...] = jnp.concatenate(outs, axis=0).T.astype(BF16)


def _dsa(qidx_t, qlat_t, widx, kidx, ckv, ckv_t, bias_tab, wuv_t, b, s):
    nq = s // Q_BLOCK
    nt = s // KEY_TILE
    top_k = min(TOPK_MAX, s // 4)
    hq = A_HEADS * Q_BLOCK
    row = lambda bi, qi: bi * nq + qi
    qspec = pl.BlockSpec((1, LANES, hq), lambda bi, qi: (row(bi, qi), 0, 0))
    kspec = pl.BlockSpec((1, nt, KEY_TILE, LANES), lambda bi, qi: (bi, 0, 0, 0))
    return pl.pallas_call(
        functools.partial(_dsa_kernel, top_k=top_k),
        grid=(b, nq),
        in_specs=[qspec, qspec,
                  pl.BlockSpec((1, IDX_HEADS, Q_BLOCK), lambda bi, qi: (row(bi, qi), 0, 0)),
                  kspec, kspec,
                  pl.BlockSpec((1, nt, KV_RANK, KEY_TILE), lambda bi, qi: (bi, 0, 0, 0)),
                  _resident((N_BIAS_TABLES, A_HEADS, KEY_TILE, Q_BLOCK)),
                  _resident((A_HEADS, A_HEAD_DIM, KV_RANK))],
        out_specs=pl.BlockSpec((Q_BLOCK, A_WIDTH), lambda bi, qi: (row(bi, qi), 0)),
        out_shape=jax.ShapeDtypeStruct((b * s, A_WIDTH), BF16),
        scratch_shapes=[pltpu.VMEM((nt, KEY_TILE, Q_BLOCK), I32),
                        pltpu.VMEM((nt, KEY_TILE, hq), F32),
                        pltpu.VMEM((KV_RANK, hq), F32)],
        compiler_params=_cparams(("parallel", "arbitrary")),
        name="dsa",
    )(qidx_t, qlat_t, widx, kidx.reshape(b, nt, KEY_TILE, LANES), ckv.reshape(b, nt, KEY_TILE, LANES),
      ckv_t.reshape(b, nt, KV_RANK, KEY_TILE), bias_tab, wuv_t)


FF_CHUNK = 256
TAIL_TILE = 512


def _mlstm_chunk(qk, sm, v_ref, o_ref, rows, gbias, ng_ref, consts, state):
    L = CHUNK
    causal, lower, upper, is_f, ones = consts
    c_st, m_st = state
    hs = range(M_HEADS)
    exact = dict(preferred_element_type=F32, precision=lax.Precision.HIGHEST)
    kt_all = qk[:, M_HEADS * M_QK:].T
    g = sm + gbias
    xg = jnp.where(is_f, jax.nn.log_sigmoid(g), g)
    rows_if = xg.T[SM_I:SM_I + 2 * M_HEADS, :]
    cum_r = jnp.dot(rows_if, upper, **exact)
    bc = jnp.dot(lower, xg, **exact)
    bc_q = [jnp.broadcast_to(bc[:, SM_F + h:SM_F + h + 1], (L, LANES)) for h in hs]
    b_r = [rows_if[h:h + 1, :] - cum_r[M_HEADS + h:M_HEADS + h + 1, :] for h in hs]
    b_last = [bc_q[h][L - 1:L, :] for h in hs]
    d_log = [jnp.where(causal, bc_q[h] + b_r[h], -jnp.inf) for h in hs]
    inter = [bc_q[h] + m_st[h] for h in hs]
    m_row = [jnp.maximum(inter[h], jnp.max(d_log[h], axis=-1, keepdims=True)) for h in hs]
    w_intra = [jnp.exp(d_log[h] - m_row[h]) for h in hs]
    w_inter = [jnp.exp(inter[h] - m_row[h]) for h in hs]
    floor = [jnp.exp(-m_row[h]) for h in hs]

    qb = [(qk[:, h * M_QK:(h + 1) * M_QK] * (M_QK ** -0.5)).astype(BF16) for h in hs]
    kt = [kt_all[h * M_QK:(h + 1) * M_QK, :] for h in hs]
    vaug = [jnp.concatenate([v_ref[rows, h * M_V:(h + 1) * M_V], ones], axis=1) for h in hs]
    sc = [_dot(qb[h], kt[h].astype(BF16)) for h in hs]
    cross = [_dot(qb[h], c_st[h].astype(BF16)) for h in hs]

    g_r = [b_last[h] + b_r[h] for h in hs]
    m_new = [jnp.maximum(b_last[h] + m_st[h], jnp.max(g_r[h], axis=-1, keepdims=True)) for h in hs]
    wk_r = [jnp.exp(g_r[h] - m_new[h]) for h in hs]
    decay = [jnp.exp(b_last[h] + m_st[h] - m_new[h]) for h in hs]
    upd = [_dot((kt[h] * wk_r[h]).astype(BF16), vaug[h]) for h in hs]

    sc = [(sc[h] * w_intra[h]).astype(BF16) for h in hs]
    num = [_dot(sc[h], vaug[h]) + jnp.concatenate([w_inter[h], w_inter[h]], axis=1) * cross[h] for h in hs]
    out = [num[h][:, :M_V] / jnp.maximum(jnp.abs(num[h][:, M_V:]), floor[h]) for h in hs]
    c_new = [jnp.concatenate([decay[h], decay[h]], axis=1) * c_st[h] + upd[h] for h in hs]
    y = [_group_norm(out[h], ng_ref[0:1, h * M_V:(h + 1) * M_V]) for h in hs]
    hm = [(y[h] * jax.nn.sigmoid(o_ref[rows, h * M_V:(h + 1) * M_V].astype(F32))).astype(BF16) for h in hs]
    return jnp.concatenate(hm, axis=1), (c_new, m_new)


def _ret_chunk(q_ref, k_ref, v_ref, g_ref, rows, cosf, sinf, intra_ref, qdec_ref, kdec_ref, cdec_ref,
               ng_ref, state):
    hs = range(R_HEADS)
    sl = [slice(h * R_QK, (h + 1) * R_QK) for h in hs]
    half = R_QK // 2

    def rot(x):
        return x * cosf + pltpu.roll(x, half, 1) * sinf

    qr = [rot(q_ref[rows, sl[h]]).astype(BF16) for h in hs]
    kt = [(rot(k_ref[rows, sl[h]]) * (R_QK ** -0.5)).T for h in hs]
    vb = [v_ref[rows, sl[h]] for h in hs]
    sc = [_dot(qr[h], kt[h].astype(BF16)) for h in hs]
    cross = [_dot(qr[h], state[h].astype(BF16)) for h in hs]
    upd = [_dot((kt[h] * kdec_ref[h:h + 1, :]).astype(BF16), vb[h]) for h in hs]
    sc = [(sc[h] * intra_ref[h]).astype(BF16) for h in hs]
    out = [_dot(sc[h], vb[h]) + qdec_ref[h] * cross[h] for h in hs]
    new = [cdec_ref[h:h + 1, :] * state[h] + upd[h] for h in hs]
    y = [_group_norm(out[h], ng_ref[0:1, sl[h]]) for h in hs]
    hr = []
    for h in hs:
        gt = g_ref[rows, sl[h]].astype(F32)
        hr.append((y[h] * (gt * jax.nn.sigmoid(gt))).astype(BF16))
    return jnp.concatenate(hr, axis=1), new


def _tail_kernel(x_ref, ha_ref, gm_ref, ga_ref, gr_ref,
                 qk_ref, sm_ref, mv_ref, mo_ref, rq_ref, rk_ref, rv_ref, rg_ref, cos_ref, sin_ref,
                 conv_ref, gb_ref, mng_ref, intra_ref, qdec_ref, kdec_ref, cdec_ref, rng_ref,
                 pm_ref, pa_ref, pr_ref, wo_ref, n2_ref, wg_ref, wu_ref, wd_ref, fg_ref,
                 out_ref, hm_scr, hr_scr, xbuf, cst, mst, rst, *, final, n_tiles, tiles_per_row):
    L = CHUNK
    TM = TAIL_TILE
    i = pl.program_id(0)
    scan_tile = jnp.minimum(i, n_tiles - 1)
    cur = lax.rem(i, 2)
    prev = lax.rem(i + 1, 2)

    @pl.when(i == 0)
    def _():
        hm_scr[...] = jnp.zeros_like(hm_scr)
        hr_scr[...] = jnp.zeros_like(hr_scr)

    @pl.when(lax.rem(scan_tile, tiles_per_row) == 0)
    def _():
        xbuf[0:SUBLANES, :] = jnp.zeros((SUBLANES, 2 * M_HEADS * M_QK), F32)
        cst[...] = jnp.zeros_like(cst)
        mst[...] = jnp.zeros_like(mst)
        rst[...] = jnp.zeros_like(rst)

    y = (jax.nn.sigmoid(gm_ref[...].astype(F32)) * _dot(hm_scr[prev], pm_ref[...])
         + jax.nn.sigmoid(ga_ref[...].astype(F32)) * _dot(ha_ref[...], pa_ref[...])
         + jax.nn.sigmoid(gr_ref[...].astype(F32)) * _dot(hr_scr[prev], pr_ref[...]))
    x1 = x_ref[...] + _dot(y.astype(BF16), wo_ref[...])
    h2 = _rms(x1, n2_ref[0:1, :]).astype(BF16)

    x_qk = qk_ref[...]
    xbuf[SUBLANES:SUBLANES + TM, :] = x_qk
    w = conv_ref[...]
    conv = w[CONV_W - 1:CONV_W, :] * x_qk
    for j in range(CONV_W - 1):
        off = SUBLANES - (CONV_W - 1) + j
        conv = conv + w[j:j + 1, :] * xbuf[off:off + TM, :]
    xbuf[0:SUBLANES, :] = x_qk[TM - SUBLANES:TM, :]
    qk_all = conv * jax.nn.sigmoid(conv)

    r_i = lax.broadcasted_iota(I32, (L, L), 0)
    c_i = lax.broadcasted_iota(I32, (L, L), 1)
    lane = lax.broadcasted_iota(I32, (L, LANES), 1)
    causal = r_i >= c_i
    consts = (causal, causal.astype(F32), (r_i <= c_i).astype(F32),
              (lane >= SM_F) & (lane < SM_F + M_HEADS), jnp.ones((L, M_V), BF16))
    gbias = gb_ref[0:1, :]
    m_state = ([cst[h] for h in range(M_HEADS)], [mst[h:h + 1, :] for h in range(M_HEADS)])
    r_state = [rst[h] for h in range(R_HEADS)]

    def scan_unit(u, m_state, r_state):
        c = u // 2
        rows = slice(c * L, (c + 1) * L)
        if u % 2 == 0:
            hm, m_state = _mlstm_chunk(qk_all[rows], sm_ref[rows, :], mv_ref, mo_ref, rows, gbias,
                                       mng_ref, consts, m_state)
            hm_scr[cur, rows, :] = hm
        else:
            hr, r_state = _ret_chunk(rq_ref, rk_ref, rv_ref, rg_ref, rows, cos_ref[rows, :],
                                     sin_ref[rows, :], intra_ref, qdec_ref, kdec_ref, cdec_ref,
                                     rng_ref, r_state)
            hr_scr[cur, rows, :] = hr
        return m_state, r_state

    n_units = 2 * (TM // L)
    n_ff = D_FF // FF_CHUNK
    assert n_units <= n_ff
    acc = x1
    for c in range(n_ff):
        cs = slice(c * FF_CHUNK, (c + 1) * FF_CHUNK)
        gate = _dot(h2, wg_ref[:, cs])
        up = _dot(h2, wu_ref[:, cs])
        act = (gate * jax.nn.sigmoid(gate) * up).astype(BF16)
        acc = acc + _dot(act, wd_ref[cs, :])
        if c < n_units:
            m_state, r_state = scan_unit(c, m_state, r_state)

    for h in range(M_HEADS):
        cst[h] = m_state[0][h]
    mst[0:M_HEADS, :] = jnp.concatenate(m_state[1], axis=0)
    for h in range(R_HEADS):
        rst[h] = r_state[h]
    if final:
        acc = _rms(acc, fg_ref[0:1, :])
    out_ref[...] = acc


def _tail(x2, ha, zf, zb, cosf, sinf, conv_w, gbias, mng, intra, qdec, kdec, cdec, rng,
          pm, pa, pr, wo, n2, wg, wu, wd, fg, s, final):
    t = x2.shape[0]
    tm = TAIL_TILE
    n_tiles = t // tm
    tiles_per_row = s // tm
    w512 = 2 * M_HEADS * M_QK
    merge_tile = lambda i: jnp.maximum(i - 1, 0)
    scan_tile = lambda i: jnp.minimum(i, n_tiles - 1)
    mspec = lambda w, cb=0: pl.BlockSpec((tm, w), lambda i: (merge_tile(i), cb))
    sspec = lambda w, off: pl.BlockSpec((tm, w), lambda i: (scan_tile(i), off // w))
    pspec = pl.BlockSpec((tm, R_QK), lambda i: (lax.rem(scan_tile(i), tiles_per_row), 0))
    return pl.pallas_call(
        functools.partial(_tail_kernel, final=final, n_tiles=n_tiles, tiles_per_row=tiles_per_row),
        grid=(n_tiles + 1,),
        in_specs=[mspec(D_MODEL), mspec(A_WIDTH),
                  mspec(D_MODEL, B_GM // D_MODEL), mspec(D_MODEL, B_GA // D_MODEL),
                  mspec(D_MODEL, B_GR // D_MODEL),
                  sspec(w512, F_MQK), sspec(LANES, F_SMALL), sspec(M_WIDTH, B_MV), sspec(M_WIDTH, B_MO),
                  sspec(R_WIDTH, F_RQ), sspec(R_WIDTH, F_RK), sspec(R_WIDTH, B_RV), sspec(R_WIDTH, B_RG),
                  pspec, pspec,
                  _resident((SUBLANES, w512)), _resident((SUBLANES, LANES)), _resident((SUBLANES, M_WIDTH)),
                  _resident((R_HEADS, CHUNK, CHUNK)), _resident((R_HEADS, CHUNK, R_V)),
                  _resident((SUBLANES, CHUNK)), _resident((SUBLANES, R_V)), _resident((SUBLANES, R_WIDTH)),
                  _resident((M_WIDTH, D_MODEL)), _resident((A_WIDTH, D_MODEL)),
                  _resident((R_WIDTH, D_MODEL)), _resident((D_MODEL, D_MODEL)),
                  _resident((SUBLANES, D_MODEL)),
                  _resident((D_MODEL, D_FF)), _resident((D_MODEL, D_FF)),
                  _resident((D_FF, D_MODEL)), _resident((SUBLANES, D_MODEL))],
        out_specs=mspec(D_MODEL),
        out_shape=jax.ShapeDtypeStruct((t, D_MODEL), F32),
        scratch_shapes=[pltpu.VMEM((2, tm, M_WIDTH), BF16),
                        pltpu.VMEM((2, tm, R_WIDTH), BF16),
                        pltpu.VMEM((SUBLANES + tm, w512), F32),
                        pltpu.VMEM((M_HEADS, M_QK, 2 * M_V), F32),
                        pltpu.VMEM((SUBLANES, LANES), F32),
                        pltpu.VMEM((R_HEADS, R_QK, R_V), F32)],
        compiler_params=_cparams(("arbitrary",), VMEM_LIMIT_TAIL),
        name="tail",
    )(x2, ha, zb, zb, zb, zf, zf, zb, zb, zf, zf, zb, zb, cosf, sinf,
      conv_w, gbias, mng, intra, qdec, kdec, cdec, rng, pm, pa, pr, wo, n2, wg, wu, wd, fg)


def _pad_rows(v, rows=SUBLANES):
    v = jnp.atleast_2d(v).astype(F32)
    return jnp.pad(v, ((0, rows - v.shape[0]), (0, 0)))


def _t5_bucket(dist):
    max_exact = N_BUCKETS // 2
    d_f = jnp.maximum(dist, 1).astype(F32)
    large = max_exact + (jnp.log(d_f / max_exact) / math.log(MAX_DISTANCE / max_exact)
                         * (N_BUCKETS - max_exact)).astype(I32)
    large = jnp.minimum(large, N_BUCKETS - 1)
    return jnp.where(dist < max_exact, dist, large)


def _bias_table(rel_bias):
    d = jnp.arange(N_BIAS_TABLES, dtype=I32)[:, None, None]
    j = jnp.arange(KEY_TILE, dtype=I32)[None, :, None]
    i = jnp.arange(Q_BLOCK, dtype=I32)[None, None, :]
    bucket = _t5_bucket(jnp.maximum(i + d * Q_BLOCK - j, 0))
    onehot = (bucket[None] == jnp.arange(N_BUCKETS, dtype=I32)[:, None, None, None]).astype(F32)
    return jnp.einsum("nh,ndkq->dhkq", rel_bias.astype(F32), onehot, precision=lax.Precision.HIGHEST)


def _rotary_tables(s):
    half = R_QK // 2
    freqs = ROPE_BASE ** (-jnp.linspace(0.0, 1.0, half, dtype=F32))
    ang = jnp.arange(s, dtype=F32)[:, None] * freqs[None, :]
    cos, sin = jnp.cos(ang), jnp.sin(ang)
    return jnp.concatenate([cos, cos], axis=1), jnp.concatenate([-sin, sin], axis=1)


def _retention_tables():
    h = R_HEADS
    log_gamma = jnp.log1p(-jnp.exp2(-5.0 - jnp.arange(h, dtype=F32)))
    pos = jnp.arange(CHUNK, dtype=F32)
    diff = pos[:, None] - pos[None, :]
    intra = jnp.where(diff >= 0, jnp.exp(jnp.maximum(diff, 0.0)[None] * log_gamma[:, None, None]), 0.0)
    q_decay = jnp.exp((pos[None, :] + 1.0) * log_gamma[:, None])
    k_decay = jnp.exp((CHUNK - 1.0 - pos[None, :]) * log_gamma[:, None])
    chunk_decay = jnp.exp(CHUNK * log_gamma)
    qdec = jnp.broadcast_to(q_decay[:, :, None], (h, CHUNK, R_V))
    kdec = _pad_rows(k_decay)
    cdec = _pad_rows(jnp.broadcast_to(chunk_decay[:, None], (h, R_V)))
    return intra, qdec, kdec, cdec


def _layout_w_in(w_in):
    parts = jnp.split(w_in, [int(c) for c in np.cumsum(IN_SPLITS)[:-1]], axis=-1)
    (m_q, m_k, m_v, m_i, m_f, m_o, a_cq, a_kidx, a_widx, a_ckv,
     r_q, r_k, r_v, r_g, g_m, g_a, g_r) = parts
    small = jnp.concatenate([a_kidx, m_i, m_f, a_widx], axis=-1)
    small = jnp.pad(small, ((0, 0), (0, LANES - small.shape[-1])))
    w = jnp.concatenate([m_q, m_k, a_cq, a_ckv, small, r_q, r_k,
                         m_v, m_o, r_v, r_g, g_m, g_a, g_r], axis=-1)
    assert w.shape[-1] == ZF_WIDTH + ZB_WIDTH
    return w.astype(BF16)


def _row_tile(t, want):
    while t % want:
        want //= 2
    return want


def kernel(x, rel_bias, final_norm_g, norm1_g, w_in, m_conv, m_ibias, m_fbias, m_norm_g, a_qnorm_g, a_wuq, a_wuq_idx, a_kidx_g, a_kvnorm_g, a_wuk, a_wuv, r_norm_g, p_m, p_a, p_r, w_out, norm2_g, w_gate, w_up, w_down):
    b, s, d = x.shape
    depth = w_in.shape[0]
    assert d == D_MODEL and s % KEY_TILE == 0 and s % TAIL_TILE == 0
    assert SM_F == SM_I + M_HEADS and M_V == LANES and R_V == LANES and CHUNK == LANES
    t = b * s
    tm_in = _row_tile(t, 2048)
    tm = _row_tile(t, 1024)

    bias_tab = _bias_table(rel_bias)
    cosf, sinf = _rotary_tables(s)
    intra, qdec, kdec, cdec = _retention_tables()
    fg = _pad_rows(final_norm_g)

    x2 = x.reshape(t, d)
    for l in range(depth):
        zf, zb = _inproj(x2, norm1_g[l][None, :], _layout_w_in(w_in[l]), tm_in, 1024)

        gbias = jnp.zeros((LANES,), F32)
        gbias = gbias.at[SM_I:SM_I + M_HEADS].set(m_ibias[l]).at[SM_F:SM_F + M_HEADS].set(m_fbias[l])

        wuqi_t = a_wuq_idx[l].reshape(Q_RANK, IDX_HEADS, IDX_DIM).transpose(1, 2, 0)
        wuqi_t = jnp.pad(wuqi_t, ((0, 0), (0, LANES - IDX_DIM), (0, 0))).astype(BF16)
        kig = jnp.pad(a_kidx_g[l], (0, LANES - IDX_DIM))
        qlat_t, qidx_t, widx, kidx, ckv, ckv_t = _dsa_prep(
            zf, _pad_rows(a_qnorm_g[l]), _pad_rows(a_kvnorm_g[l]), _pad_rows(kig),
            a_wuq[l].T.astype(BF16), wuqi_t, jnp.swapaxes(a_wuk[l], 1, 2).astype(BF16), tm)
        ha = _dsa(qidx_t, qlat_t, widx, kidx, ckv, ckv_t, bias_tab,
                  jnp.swapaxes(a_wuv[l], 1, 2).astype(BF16), b, s)

        x2 = _tail(x2, ha, zf, zb, cosf, sinf, _pad_rows(m_conv[l]), _pad_rows(gbias),
                   _pad_rows(m_norm_g[l]), intra, qdec, kdec, cdec, _pad_rows(r_norm_g[l]),
                   p_m[l].astype(BF16), p_a[l].astype(BF16), p_r[l].astype(BF16),
                   w_out[l].astype(BF16), _pad_rows(norm2_g[l]), w_gate[l].astype(BF16),
                   w_up[l].astype(BF16), w_down[l].astype(BF16), fg, s, final=(l == depth - 1))
    return x2.reshape(b, s, d)
```
